```python
import jax, jax.numpy as jnp
from jax import lax
import numpy as np

D_MODEL = 1024
BATCH = 8
SEQ = 2048
DEPTH = 2
DEC_BATCH = 128
DEC_SEQ = 4
PAST_LEN = 16384
PAGE_SIZE = 128

D_RNN = D_MODEL
N_LRU_BLOCKS = 16
LRU_BLOCK = D_RNN // N_LRU_BLOCKS
LRU_CONV_W = 4
LRU_C = 8.0
D_CONV = D_MODEL
CONF_CONV_W = 31
N_EXPERTS = 16
N_GROUPS = 4
EXPERTS_PER_GROUP = N_EXPERTS // N_GROUPS
TOP_K = 2
D_EXPERT = D_MODEL // 2
N_IN = 2 * D_RNN + 2 * D_CONV + 2 * D_MODEL
SPLITS = [D_RNN, 2 * D_RNN, 2 * D_RNN + D_CONV, 2 * D_RNN + 2 * D_CONV, 2 * D_RNN + 2 * D_CONV + D_MODEL]
RMS_EPS = 1e-6
LN_EPS = 1e-5

kernel_name = 'hawk_conformer_moe_adaln_step'


def rmsnorm(x, g):
    xf = x.astype(jnp.float32)
    y = xf * lax.rsqrt(jnp.mean(xf * xf, axis=-1, keepdims=True) + RMS_EPS)
    return (y * g.astype(jnp.float32)).astype(x.dtype)


def layernorm(x, g, b):
    xf = x.astype(jnp.float32)
    mu = jnp.mean(xf, axis=-1, keepdims=True)
    xc = xf - mu
    var = jnp.mean(xc * xc, axis=-1, keepdims=True)
    y = xc * lax.rsqrt(var + LN_EPS) * g.astype(jnp.float32) + b.astype(jnp.float32)
    return y.astype(x.dtype)


def causal_dwconv(u, buf, w, b):
    width = w.shape[0]
    full = jnp.concatenate([buf.astype(u.dtype), u], axis=1)
    y = lax.conv_general_dilated(full, w[:, None, :].astype(u.dtype), window_strides=(1,),
                                 padding='VALID', dimension_numbers=('NWC', 'WIO', 'NWC'),
                                 feature_group_count=u.shape[-1])
    return y + b.astype(u.dtype), full[:, -(width - 1):]


def rg_lru(x, h0, start, wa, ba, wx, bx, lam):
    bsz, t_len, ch = x.shape
    f32 = jnp.float32
    xf = x.astype(f32)
    xb = xf.reshape(bsz, t_len, N_LRU_BLOCKS, LRU_BLOCK)
    r = jax.nn.sigmoid(jnp.einsum('bthi,hij->bthj', xb, wa.astype(f32)).reshape(bsz, t_len, ch) + ba.astype(f32))
    i = jax.nn.sigmoid(jnp.einsum('bthi,hij->bthj', xb, wx.astype(f32)).reshape(bsz, t_len, ch) + bx.astype(f32))
    log_a = -LRU_C * r * jax.nn.softplus(-lam.astype(f32))
    a = jnp.exp(log_a)
    mult = jnp.sqrt(-jnp.expm1(2.0 * log_a))
    reset = ((start + jnp.arange(t_len)) == 0)[None, :, None]
    a = jnp.where(reset, 0.0, a)
    mult = jnp.where(reset, 1.0, mult)
    bterm = mult * i * xf
    bterm = bterm.at[:, 0].add(a[:, 0] * h0.astype(f32))

    def comb(left, right):
        a1, b1 = left
        a2, b2 = right
        return a1 * a2, a2 * b1 + b2

    _, h = lax.associative_scan(comb, (a, bterm), axis=1)
    return h.astype(x.dtype), h[:, -1]


def moe(h, w_router, b_router, w_gate, w_up, w_down):
    bsz, t_len, d = h.shape
    hf = h.reshape(bsz * t_len, d)
    logits = hf.astype(jnp.float32) @ w_router.astype(jnp.float32) + b_router.astype(jnp.float32)
    probs = jax.nn.softmax(logits, axis=-1)
    grp_top, _ = lax.top_k(probs.reshape(-1, N_GROUPS, EXPERTS_PER_GROUP), TOP_K)
    grp = jnp.argmax(grp_top.sum(-1), axis=-1)
    in_grp = (jnp.arange(N_EXPERTS) // EXPERTS_PER_GROUP)[None, :] == grp[:, None]
    vals, idx = lax.top_k(jnp.where(in_grp, probs, -1.0), TOP_K)
    wts = vals / jnp.sum(vals, axis=-1, keepdims=True)
    combine = jnp.sum(jax.nn.one_hot(idx, N_EXPERTS, dtype=jnp.float32) * wts[..., None], axis=1)
    out = jnp.zeros(hf.shape, jnp.float32)
    for e in range(N_EXPERTS):
        ye = (jax.nn.silu(hf @ w_gate[e]) * (hf @ w_up[e])) @ w_down[e]
        out = out + combine[:, e:e + 1] * ye.astype(jnp.float32)
    return out.astype(h.dtype).reshape(bsz, t_len, d)


def layer(x, c, start, h0, lru_buf, conf_buf, w_ada, b_ada, g_mix, g_ffn, w_in, b_in,
          lru_conv_w, lru_conv_b, lru_wa, lru_ba, lru_wx, lru_bx, lru_lambda, w_lru_o,
          conf_conv_w, conf_conv_b, conf_ln_g, conf_ln_b, w_conf_o, w_out,
          w_router, b_router, w_gate, w_up, w_down):
    mod = jax.nn.silu(c) @ w_ada + b_ada
    sh1, sc1, gt1, sh2, sc2, gt2 = jnp.split(mod[:, None, :], 6, axis=-1)
    hm = rmsnorm(x, g_mix) * (1.0 + sc1) + sh1
    proj = hm @ w_in + b_in
    u_lru, gate_lru, conf_a, conf_b, bg_lru, bg_conf = jnp.split(proj, SPLITS, axis=-1)
    xc, new_lru_buf = causal_dwconv(u_lru, lru_buf, lru_conv_w, lru_conv_b)
    hseq, h_last = rg_lru(xc, h0, start, lru_wa, lru_ba, lru_wx, lru_bx, lru_lambda)
    out_a = (hseq * jax.nn.gelu(gate_lru)) @ w_lru_o
    glu = conf_a * jax.nn.sigmoid(conf_b)
    dc, new_conf_buf = causal_dwconv(glu, conf_buf, conf_conv_w, conf_conv_b)
    out_b = jax.nn.silu(layernorm(dc, conf_ln_g, conf_ln_b)) @ w_conf_o
    merged = jax.nn.sigmoid(bg_lru) * out_a + jax.nn.sigmoid(bg_conf) * out_b
    x = x + gt1 * (merged @ w_out)
    hf = rmsnorm(x, g_ffn) * (1.0 + sc2) + sh2
    x = x + gt2 * moe(hf, w_router, b_router, w_gate, w_up, w_down)
    return x, h_last, new_lru_buf, new_conf_buf


def setup_inputs(seed: int = 0) -> dict:
    key = jax.random.key(seed)
    ks = jax.random.split(key, 40)
    f32 = jnp.float32
    nrm = lambda k, s, sc: jax.random.normal(k, s, f32) * sc
    a0 = jax.random.uniform(ks[20], (DEPTH, D_RNN), f32, 0.9, 0.999)
    s = a0 ** (1.0 / LRU_C)
    lam = jnp.log(s) - jnp.log1p(-s)
    return {
        'x_prompt': nrm(ks[0], (BATCH, SEQ, D_MODEL), 1.0),
        'x_sample': nrm(ks[1], (DEC_BATCH, DEC_SEQ, D_MODEL), 1.0),
        'state_lru_h': nrm(ks[2], (DEPTH, DEC_BATCH, D_RNN), 0.5),
        'state_lru_conv': nrm(ks[3], (DEPTH, DEC_BATCH, LRU_CONV_W - 1, D_RNN), 1.0),
        'state_conf_conv': nrm(ks[4], (DEPTH, DEC_BATCH, CONF_CONV_W - 1, D_CONV), 1.0),
        'c_prompt': nrm(ks[5], (BATCH, D_MODEL), 1.0),
        'c_sample': nrm(ks[6], (DEC_BATCH, D_MODEL), 1.0),
        'w_ada': nrm(ks[7], (DEPTH, D_MODEL, 6 * D_MODEL), 0.5 * D_MODEL ** -0.5),
        'b_ada': nrm(ks[8], (DEPTH, 6 * D_MODEL), 0.02),
        'g_mix': 1.0 + nrm(ks[9], (DEPTH, D_MODEL), 0.05),
        'g_ffn': 1.0 + nrm(ks[10], (DEPTH, D_MODEL), 0.05),
        'w_in': nrm(ks[11], (DEPTH, D_MODEL, N_IN), D_MODEL ** -0.5),
        'b_in': nrm(ks[12], (DEPTH, N_IN), 0.02),
        'lru_conv_w': nrm(ks[13], (DEPTH, LRU_CONV_W, D_RNN), LRU_CONV_W ** -0.5),
        'lru_conv_b': nrm(ks[14], (DEPTH, D_RNN), 0.02),
        'lru_wa': nrm(ks[15], (DEPTH, N_LRU_BLOCKS, LRU_BLOCK, LRU_BLOCK), LRU_BLOCK ** -0.5),
        'lru_ba': nrm(ks[16], (DEPTH, D_RNN), 0.02),
        'lru_wx': nrm(ks[17], (DEPTH, N_LRU_BLOCKS, LRU_BLOCK, LRU_BLOCK), LRU_BLOCK ** -0.5),
        'lru_bx': nrm(ks[18], (DEPTH, D_RNN), 0.02),
        'lru_lambda': lam,
        'w_lru_o': nrm(ks[19], (DEPTH, D_RNN, D_MODEL), D_RNN ** -0.5),
        'conf_conv_w': nrm(ks[21], (DEPTH, CONF_CONV_W, D_CONV), CONF_CONV_W ** -0.5),
        'conf_conv_b': nrm(ks[22], (DEPTH, D_CONV), 0.02),
        'conf_ln_g': 1.0 + nrm(ks[23], (DEPTH, D_CONV), 0.05),
        'conf_ln_b': nrm(ks[24], (DEPTH, D_CONV), 0.02),
        'w_conf_o': nrm(ks[25], (DEPTH, D_CONV, D_MODEL), D_CONV ** -0.5),
        'w_out': nrm(ks[26], (DEPTH, D_MODEL, D_MODEL), D_MODEL ** -0.5),
        'w_router': nrm(ks[27], (D_MODEL, N_EXPERTS), D_MODEL ** -0.5),
        'b_router': nrm(ks[28], (N_EXPERTS,), 0.01),
        'w_gate': nrm(ks[29], (DEPTH, N_EXPERTS, D_MODEL, D_EXPERT), D_MODEL ** -0.5),
        'w_up': nrm(ks[30], (DEPTH, N_EXPERTS, D_MODEL, D_EXPERT), D_MODEL ** -0.5),
        'w_down': nrm(ks[31], (DEPTH, N_EXPERTS, D_EXPERT, D_MODEL), D_EXPERT ** -0.5),
        'g_final': 1.0 + nrm(ks[32], (D_MODEL,), 0.05),
    }


def reference(x_prompt, x_sample, state_lru_h, state_lru_conv, state_conf_conv, c_prompt, c_sample,
              w_ada, b_ada, g_mix, g_ffn, w_in, b_in, lru_conv_w, lru_conv_b, lru_wa, lru_ba,
              lru_wx, lru_bx, lru_lambda, w_lru_o, conf_conv_w, conf_conv_b, conf_ln_g, conf_ln_b,
              w_conf_o, w_out, w_router, b_router, w_gate, w_up, w_down, g_final):
    xp, xs = x_prompt, x_sample
    hp_list, lcp_list, ccp_list = [], [], []
    hs_list, lcs_list, ccs_list = [], [], []
    for l in range(DEPTH):
        lw = (w_ada[l], b_ada[l], g_mix[l], g_ffn[l], w_in[l], b_in[l], lru_conv_w[l], lru_conv_b[l],
              lru_wa[l], lru_ba[l], lru_wx[l], lru_bx[l], lru_lambda[l], w_lru_o[l],
              conf_conv_w[l], conf_conv_b[l], conf_ln_g[l], conf_ln_b[l], w_conf_o[l], w_out[l],
              w_router, b_router, w_gate[l], w_up[l], w_down[l])
        h0p = jnp.zeros((BATCH, D_RNN), jnp.float32)
        lbp = jnp.zeros((BATCH, LRU_CONV_W - 1, D_RNN), xp.dtype)
        cbp = jnp.zeros((BATCH, CONF_CONV_W - 1, D_CONV), xp.dtype)
        xp, hp, lcp, ccp = layer(xp, c_prompt, 0, h0p, lbp, cbp, *lw)
        xs, hs, lcs, ccs = layer(xs, c_sample, PAST_LEN, state_lru_h[l], state_lru_conv[l],
                                 state_conf_conv[l], *lw)
        hp_list.append(hp); lcp_list.append(lcp); ccp_list.append(ccp)
        hs_list.append(hs); lcs_list.append(lcs); ccs_list.append(ccs)
    y_prompt = rmsnorm(xp, g_final)
    y_sample = rmsnorm(xs, g_final)
    return (y_prompt, y_sample,
            jnp.stack(hp_list), jnp.stack(lcp_list), jnp.stack(ccp_list),
            jnp.stack(hs_list), jnp.stack(lcs_list), jnp.stack(ccs_list))
```

```python
import functools

import jax
import jax.numpy as jnp
from jax import lax
from jax.experimental import pallas as pl
from jax.experimental.pallas import tpu as pltpu

F32 = jnp.float32
BF16 = jnp.bfloat16

D = 1024
DEPTH = 2
N_EXPERTS = 16
EXPERTS_PER_GROUP = 4
N_GROUPS = N_EXPERTS // EXPERTS_PER_GROUP
D_EXPERT = D // 2
LRU_TAPS = 4
CONF_TAPS = 31
LRU_C = 8.0
RMS_EPS = 1e-6
LN_EPS = 1e-5
GATE_BLOCK = 256
N_GATE_BLOCKS = D // GATE_BLOCK
MIX_ROWS = 512
MAX_BATCH_TILE = 32
MOE_ROWS = 1024
CONV_CHUNK_ROWS = 128
CONV_CHUNK_LANES = 256
VMEM_LIMIT = 56 * 1024 * 1024


def _sigmoid(x):
    return 1.0 / (1.0 + jnp.exp(-x))


def _silu(x):
    return x * _sigmoid(x)


def _gelu_tanh(x):
    c = 0.7978845608028654
    return 0.5 * x * (1.0 + jnp.tanh(c * (x + 0.044715 * (x * x * x))))


def _const_spec(shape):
    zeros = (0,) * len(shape)
    return pl.BlockSpec(shape, lambda *_: zeros, pipeline_mode=pl.Buffered(1))


def _ada_kernel(c_ref, w_ref, b_ref, o_ref):
    s = _silu(c_ref[...]).astype(BF16)
    o_ref[0] = jnp.dot(s, w_ref[0], preferred_element_type=F32) + b_ref[0]


def _ada_call(c_all, w_ada, b_ada):
    nb = c_all.shape[0]
    tn = 1536
    return pl.pallas_call(
        _ada_kernel,
        out_shape=jax.ShapeDtypeStruct((DEPTH, nb, 6 * D), F32),
        grid=(DEPTH, 6 * D // tn),
        in_specs=[
            pl.BlockSpec((nb, D), lambda l, j: (0, 0)),
            pl.BlockSpec((1, D, tn), lambda l, j: (l, 0, j)),
            pl.BlockSpec((1, 1, tn), lambda l, j: (l, 0, j)),
        ],
        out_specs=pl.BlockSpec((1, nb, tn), lambda l, j: (l, 0, j)),
        name="adaln",
    )(c_all, w_ada, b_ada)


def _depthwise_conv(full_ref, w_ref, out_ref, *, taps, t_tile, batch):
    tc = max(1, CONV_CHUNK_ROWS // batch)
    n_chunks = t_tile // tc

    def chunk(c, carry):
        t0 = c * tc
        for l0 in range(0, D, CONV_CHUNK_LANES):
            lanes = slice(l0, l0 + CONV_CHUNK_LANES)
            acc = jnp.zeros((tc, batch, CONV_CHUNK_LANES), F32)
            for k in range(taps):
                acc = acc + full_ref[pl.ds(t0 + k, tc), :, lanes] * w_ref[k, :, lanes]
            out_ref[pl.ds(t0, tc), :, lanes] = acc
        return carry

    lax.fori_loop(0, n_chunks, chunk, 0)


def _route(logits):
    m = logits[0]
    for e in range(1, N_EXPERTS):
        m = jnp.maximum(m, logits[e])
    ex = [jnp.exp(l - m) for l in logits]
    den = ex[0]
    for e in range(1, N_EXPERTS):
        den = den + ex[e]
    p = [v / den for v in ex]
    sums = []
    for g in range(N_GROUPS):
        q = p[g * EXPERTS_PER_GROUP:(g + 1) * EXPERTS_PER_GROUP]
        m1 = jnp.maximum(jnp.maximum(q[0], q[1]), jnp.maximum(q[2], q[3]))
        m2 = None
        for i in range(EXPERTS_PER_GROUP):
            for j in range(i + 1, EXPERTS_PER_GROUP):
                mn = jnp.minimum(q[i], q[j])
                m2 = mn if m2 is None else jnp.maximum(m2, mn)
        sums.append(m1 + m2)
    best = sums[0]
    grp = jnp.zeros(best.shape, jnp.int32)
    for g in range(1, N_GROUPS):
        take = sums[g] > best
        best = jnp.where(take, sums[g], best)
        grp = jnp.where(take, g, grp)
    combine = []
    for g in range(N_GROUPS):
        q = p[g * EXPERTS_PER_GROUP:(g + 1) * EXPERTS_PER_GROUP]
        in_grp = grp == g
        for i in range(EXPERTS_PER_GROUP):
            rank = jnp.zeros(best.shape, jnp.int32)
            for j in range(EXPERTS_PER_GROUP):
                if j == i:
                    continue
                ahead = (q[j] > q[i]) if j > i else (q[j] >= q[i])
                rank = rank + ahead.astype(jnp.int32)
            sel = jnp.where(in_grp, rank, 2) < 2
            combine.append(jnp.where(sel, q[i] / best, 0.0))
    return combine


def _mix_kernel(x_ref, mod_ref, h0_ref, lbuf_ref, cbuf_ref,
                gmix_ref, gffn_ref, win_ref, bin_ref,
                lcw_ref, lcb_ref, wa_ref, ba_ref, wx_ref, bx_ref, lam_ref, wlo_ref,
                ccw_ref, ccb_ref, lng_ref, lnb_ref, wco_ref, wout_ref,
                wrh_ref, wrl_ref, br_ref,
                x1_ref, hf_ref, comb_ref, hlast_ref, nlbuf_ref, ncbuf_ref,
                fu_ref, fg_ref, a_ref, b_ref, hcar_ref,
                *, t_tile, batch, reset):
    j = pl.program_id(1)
    n_t = pl.num_programs(1)
    rows = t_tile * batch

    @pl.when(j == 0)
    def _():
        fu_ref[0:LRU_TAPS - 1] = lbuf_ref[0]
        fg_ref[0:CONF_TAPS - 1] = cbuf_ref[0]
        hcar_ref[...] = h0_ref[0]

    x = x_ref[0]
    sh1, sc1, gt1 = mod_ref[0], mod_ref[1], mod_ref[2]
    sh2, sc2 = mod_ref[3], mod_ref[4]

    ms = jnp.mean(x * x, axis=-1, keepdims=True)
    hm = (x * lax.rsqrt(ms + RMS_EPS) * gmix_ref[...]) * (1.0 + sc1) + sh1
    hm2 = hm.reshape(rows, D).astype(BF16)

    def proj(k):
        cols = slice(k * D, (k + 1) * D)
        return jnp.dot(hm2, win_ref[:, cols], preferred_element_type=F32) + bin_ref[:, cols]

    fu_ref[LRU_TAPS - 1:] = proj(0).reshape(t_tile, batch, D)
    _depthwise_conv(fu_ref, lcw_ref, a_ref, taps=LRU_TAPS, t_tile=t_tile, batch=batch)
    xc = a_ref[...].reshape(rows, D) + lcb_ref[...]
    xcb = xc.astype(BF16)
    r_parts, i_parts = [], []
    for blk in range(N_GATE_BLOCKS):
        cols = slice(blk * GATE_BLOCK, (blk + 1) * GATE_BLOCK)
        r_parts.append(jnp.dot(xcb[:, cols], wa_ref[blk], preferred_element_type=F32))
        i_parts.append(jnp.dot(xcb[:, cols], wx_ref[blk], preferred_element_type=F32))
    r_gate = _sigmoid(jnp.concatenate(r_parts, axis=-1) + ba_ref[...])
    i_gate = _sigmoid(jnp.concatenate(i_parts, axis=-1) + bx_ref[...])
    neg_lam = -lam_ref[...]
    softplus = jnp.maximum(neg_lam, 0.0) + jnp.log1p(jnp.exp(-jnp.abs(neg_lam)))
    log_a = (-LRU_C) * r_gate * softplus
    a = jnp.exp(log_a)
    mult = jnp.sqrt(-jnp.tanh(log_a) * (a * a + 1.0))
    a = a.reshape(t_tile, batch, D)
    mult = mult.reshape(t_tile, batch, D)
    if reset:
        t_idx = lax.broadcasted_iota(jnp.int32, (t_tile, batch, D), 0)
        first = jnp.logical_and(t_idx == 0, j == 0)
        a = jnp.where(first, 0.0, a)
        mult = jnp.where(first, 1.0, mult)
    a_ref[...] = a
    b_ref[...] = mult * (i_gate * xc).reshape(t_tile, batch, D)

    def scan_step(t, carry):
        h = a_ref[t] * hcar_ref[...] + b_ref[t]
        hcar_ref[...] = h
        b_ref[t] = h
        return carry

    lax.fori_loop(0, t_tile, scan_step, 0)
    hseq = b_ref[...].reshape(rows, D)
    out_a = jnp.dot((hseq * _gelu_tanh(proj(1))).astype(BF16), wlo_ref[...],
                    preferred_element_type=F32)

    glu = proj(2) * _sigmoid(proj(3))
    fg_ref[CONF_TAPS - 1:] = glu.reshape(t_tile, batch, D)
    _depthwise_conv(fg_ref, ccw_ref, a_ref, taps=CONF_TAPS, t_tile=t_tile, batch=batch)
    dc = a_ref[...].reshape(rows, D) + ccb_ref[...]
    mu = jnp.mean(dc, axis=-1, keepdims=True)
    dcc = dc - mu
    var = jnp.mean(dcc * dcc, axis=-1, keepdims=True)
    ln = dcc * lax.rsqrt(var + LN_EPS) * lng_ref[...] + lnb_ref[...]
    out_b = jnp.dot(_silu(ln).astype(BF16), wco_ref[...], preferred_element_type=F32)

    merged = _sigmoid(proj(4)) * out_a + _sigmoid(proj(5)) * out_b
    mo = jnp.dot(merged.astype(BF16), wout_ref[...], preferred_element_type=F32)
    x1 = x + gt1 * mo.reshape(t_tile, batch, D)
    x1_ref[0] = x1

    ms2 = jnp.mean(x1 * x1, axis=-1, keepdims=True)
    hf = ((x1 * lax.rsqrt(ms2 + RMS_EPS) * gffn_ref[...]) * (1.0 + sc2) + sh2).reshape(rows, D)
    hf_hi = hf.astype(BF16)
    hf_ref[0] = hf_hi
    hf_lo = (hf - hf_hi.astype(F32)).astype(BF16)
    nt = (((1,), (1,)), ((), ()))
    logits = (lax.dot_general(wrh_ref[...], hf_hi, nt, preferred_element_type=F32)
              + lax.dot_general(wrh_ref[...], hf_lo, nt, preferred_element_type=F32)
              + lax.dot_general(wrl_ref[...], hf_hi, nt, preferred_element_type=F32)
              + br_ref[...])
    combine = _route([logits[e:e + 1, :] for e in range(N_EXPERTS)])
    comb_ref[0, 0] = jnp.concatenate(combine, axis=0)

    fu_ref[0:LRU_TAPS - 1] = fu_ref[t_tile:t_tile + LRU_TAPS - 1]
    fg_ref[0:CONF_TAPS - 1] = fg_ref[t_tile:t_tile + CONF_TAPS - 1]

    @pl.when(j == n_t - 1)
    def _():
        hlast_ref[0] = hcar_ref[...]
        nlbuf_ref[0] = fu_ref[0:LRU_TAPS - 1]
        ncbuf_ref[0] = fg_ref[0:CONF_TAPS - 1]


def _mix_call(x, mod, h0, lbuf, cbuf, lw, *, reset):
    nb, t_len, bt, _ = x.shape
    t_tile = min(t_len, MIX_ROWS // bt)
    n_t = t_len // t_tile
    rows = t_tile * bt
    kern = functools.partial(_mix_kernel, t_tile=t_tile, batch=bt, reset=reset)
    consts = [lw["g_mix"], lw["g_ffn"], lw["w_in"], lw["b_in"],
              lw["lru_conv_w"], lw["lru_conv_b"], lw["wa"], lw["lru_ba"], lw["wx"], lw["lru_bx"],
              lw["lru_lambda"], lw["w_lru_o"], lw["conf_conv_w"], lw["conf_conv_b"],
              lw["conf_ln_g"], lw["conf_ln_b"], lw["w_conf_o"], lw["w_out"],
              lw["wr_hi"], lw["wr_lo"], lw["b_router"]]
    in_specs = [
        pl.BlockSpec((1, t_tile, bt, D), lambda b, j: (b, j, 0, 0)),
        pl.BlockSpec((6, bt, D), lambda b, j: (0, b, 0)),
        pl.BlockSpec((1, bt, D), lambda b, j: (b, 0, 0)),
        pl.BlockSpec((1, LRU_TAPS - 1, bt, D), lambda b, j: (b, 0, 0, 0)),
        pl.BlockSpec((1, CONF_TAPS - 1, bt, D), lambda b, j: (b, 0, 0, 0)),
    ]
    in_specs += [_const_spec(c.shape) for c in consts]
    out_shape = (
        jax.ShapeDtypeStruct((nb, t_len, bt, D), F32),
        jax.ShapeDtypeStruct((nb * n_t, rows, D), BF16),
        jax.ShapeDtypeStruct((nb, n_t, N_EXPERTS, rows), F32),
        jax.ShapeDtypeStruct((nb, bt, D), F32),
        jax.ShapeDtypeStruct((nb, LRU_TAPS - 1, bt, D), F32),
        jax.ShapeDtypeStruct((nb, CONF_TAPS - 1, bt, D), F32),
    )
    out_specs = (
        pl.BlockSpec((1, t_tile, bt, D), lambda b, j: (b, j, 0, 0)),
        pl.BlockSpec((1, rows, D), lambda b, j: (b * n_t + j, 0, 0)),
        pl.BlockSpec((1, 1, N_EXPERTS, rows), lambda b, j: (b, j, 0, 0)),
        pl.BlockSpec((1, bt, D), lambda b, j: (b, 0, 0)),
        pl.BlockSpec((1, LRU_TAPS - 1, bt, D), lambda b, j: (b, 0, 0, 0)),
        pl.BlockSpec((1, CONF_TAPS - 1, bt, D), lambda b, j: (b, 0, 0, 0)),
    )
    scratch = [
        pltpu.VMEM((t_tile + LRU_TAPS - 1, bt, D), F32),
        pltpu.VMEM((t_tile + CONF_TAPS - 1, bt, D), F32),
        pltpu.VMEM((t_tile, bt, D), F32),
        pltpu.VMEM((t_tile, bt, D), F32),
        pltpu.VMEM((bt, D), F32),
    ]
    return pl.pallas_call(
        kern,
        out_shape=out_shape,
        grid=(nb, n_t),
        in_specs=in_specs,
        out_specs=out_specs,
        scratch_shapes=scratch,
        compiler_params=pltpu.CompilerParams(
            dimension_semantics=("arbitrary", "arbitrary"), vmem_limit_bytes=VMEM_LIMIT),
        name="mix_reset" if reset else "mix_cont",
    )(x, mod, h0, lbuf, cbuf, *consts)


def _moe_call(hf, comb, x1, mod, lw, g_final, *, final):
    nb, t_len, bt, _ = x1.shape
    if nb * t_len * bt <= MOE_ROWS:
        nbt, t_tile = nb, t_len
    else:
        nbt, t_tile = 1, MOE_ROWS // bt
    rows = nbt * t_tile * bt
    grid = (nb // nbt, t_len // t_tile, N_EXPERTS)
    n_t = grid[1]
    return pl.pallas_call(
        functools.partial(_moe_kernel, final=final),
        out_shape=jax.ShapeDtypeStruct((nb, t_len, bt, D), F32),
        grid=grid,
        in_specs=[
            pl.BlockSpec((rows, D), lambda b, j, e: (b * n_t + j, 0)),
            pl.BlockSpec((rows, N_EXPERTS), lambda b, j, e: (b * n_t + j, 0)),
            pl.BlockSpec((nbt, t_tile, bt, D), lambda b, j, e: (b, j, 0, 0)),
            pl.BlockSpec((6, nbt * bt, D), lambda b, j, e: (0, b, 0)),
            pl.BlockSpec((1, D, D_EXPERT), lambda b, j, e: (e, 0, 0)),
            pl.BlockSpec((1, D, D_EXPERT), lambda b, j, e: (e, 0, 0)),
            pl.BlockSpec((1, D_EXPERT, D), lambda b, j, e: (e, 0, 0)),
            pl.BlockSpec((1, D), lambda b, j, e: (0, 0)),
        ],
        out_specs=pl.BlockSpec((nbt, t_tile, bt, D), lambda b, j, e: (b, j, 0, 0)),
        scratch_shapes=[pltpu.VMEM((rows, D), F32)],
        compiler_params=pltpu.CompilerParams(
            dimension_semantics=("arbitrary", "arbitrary", "arbitrary"),
            vmem_limit_bytes=VMEM_LIMIT),
        name="moe_final" if final else "moe",
    )(hf, comb, x1, mod, lw["w_gate"], lw["w_up"], lw["w_down"], g_final)


def _moe_kernel(hf_ref, comb_ref, x1_ref, mod_ref, wg_ref, wu_ref, wd_ref, gfin_ref,
                o_ref, acc_ref, *, final):
    e = pl.program_id(2)

    @pl.when(e == 0)
    def _():
        acc_ref[...] = jnp.zeros_like(acc_ref)

    h = hf_ref[...]
    g = jnp.dot(h, wg_ref[0], preferred_element_type=F32)
    u = jnp.dot(h, wu_ref[0], preferred_element_type=F32)
    y = jnp.dot((_silu(g) * u).astype(BF16), wd_ref[0], preferred_element_type=F32)
    comb = comb_ref[...]
    lane = lax.broadcasted_iota(jnp.int32, comb.shape, 1)
    c = jnp.sum(jnp.where(lane == e, comb, 0.0), axis=-1, keepdims=True)
    acc_ref[...] += c * y

    @pl.when(e == N_EXPERTS - 1)
    def _():
        nbt, t_tile, bt, _ = x1_ref.shape
        gt2 = mod_ref[5].reshape(nbt, 1, bt, D)
        x2 = x1_ref[...] + gt2 * acc_ref[...].reshape(nbt, t_tile, bt, D)
        if final:
            ms = jnp.mean(x2 * x2, axis=-1, keepdims=True)
            x2 = x2 * lax.rsqrt(ms + RMS_EPS) * gfin_ref[...]
        o_ref[...] = x2


def _block_diag(w):
    heads_per_block = GATE_BLOCK // w.shape[-1]
    w = w.reshape(N_GATE_BLOCKS, heads_per_block, w.shape[-2], w.shape[-1])
    eye = jnp.eye(heads_per_block, dtype=w.dtype)
    out = jnp.einsum("ghij,hk->ghikj", w, eye)
    return out.reshape(N_GATE_BLOCKS, GATE_BLOCK, GATE_BLOCK).astype(BF16)


def _to_tiled(x, bt):
    b, t, d = x.shape
    return x.reshape(b // bt, bt, t, d).transpose(0, 2, 1, 3)


def _from_tiled(x):
    nb, t, bt, d = x.shape
    return x.transpose(0, 2, 1, 3).reshape(nb * bt, t, d)


def kernel(x_prompt, x_sample, state_lru_h, state_lru_conv, state_conf_conv, c_prompt, c_sample, w_ada, b_ada, g_mix, g_ffn, w_in, b_in, lru_conv_w, lru_conv_b, lru_wa, lru_ba, lru_wx, lru_bx, lru_lambda, w_lru_o, conf_conv_w, conf_conv_b, conf_ln_g, conf_ln_b, w_conf_o, w_out, w_router, b_router, w_gate, w_up, w_down, g_final):
    bp = x_prompt.shape[0]
    bs = x_sample.shape[0]
    btp = min(bp, MAX_BATCH_TILE)
    bts = min(bs, MAX_BATCH_TILE)
    row = lambda v: v.reshape(1, -1)

    c_all = jnp.concatenate([c_prompt, c_sample], axis=0)
    mod = _ada_call(c_all, w_ada.astype(BF16), b_ada.reshape(DEPTH, 1, 6 * D))
    mod = mod.reshape(DEPTH, bp + bs, 6, D).transpose(0, 2, 1, 3)

    wr_t = w_router.T
    wr_hi = wr_t.astype(BF16)
    wr_lo = (wr_t - wr_hi.astype(F32)).astype(BF16)
    g_fin = row(g_final)

    xp = _to_tiled(x_prompt, btp)
    xs = _to_tiled(x_sample, bts)

    outs_p, outs_s = [], []
    for l in range(DEPTH):
        lw = {
            "g_mix": row(g_mix[l]), "g_ffn": row(g_ffn[l]),
            "w_in": w_in[l].astype(BF16), "b_in": row(b_in[l]),
            "lru_conv_w": lru_conv_w[l][:, None, :], "lru_conv_b": row(lru_conv_b[l]),
            "wa": _block_diag(lru_wa[l]), "lru_ba": row(lru_ba[l]),
            "wx": _block_diag(lru_wx[l]), "lru_bx": row(lru_bx[l]),
            "lru_lambda": row(lru_lambda[l]), "w_lru_o": w_lru_o[l].astype(BF16),
            "conf_conv_w": conf_conv_w[l][:, None, :], "conf_conv_b": row(conf_conv_b[l]),
            "conf_ln_g": row(conf_ln_g[l]), "conf_ln_b": row(conf_ln_b[l]),
            "w_conf_o": w_conf_o[l].astype(BF16), "w_out": w_out[l].astype(BF16),
            "wr_hi": wr_hi, "wr_lo": wr_lo, "b_router": b_router.reshape(N_EXPERTS, 1),
            "w_gate": w_gate[l].astype(BF16), "w_up": w_up[l].astype(BF16),
            "w_down": w_down[l].astype(BF16),
        }
        final = l == DEPTH - 1
        groups = (
            (xp, mod[l, :, :bp], jnp.zeros((bp // btp, btp, D), F32),
             jnp.zeros((bp // btp, LRU_TAPS - 1, btp, D), F32),
             jnp.zeros((bp // btp, CONF_TAPS - 1, btp, D), F32), True),
            (xs, mod[l, :, bp:], state_lru_h[l].reshape(bs // bts, bts, D),
             _to_tiled(state_lru_conv[l], bts), _to_tiled(state_conf_conv[l], bts), False),
        )
        new_x = []
        for (xg, modg, h0, lbuf, cbuf, reset), outs in zip(groups, (outs_p, outs_s)):
            x1, hf, comb, hlast, nlbuf, ncbuf = _mix_call(xg, modg, h0, lbuf, cbuf, lw, reset=reset)
            hf = hf.reshape(-1, D)
            comb = comb.transpose(0, 1, 3, 2).reshape(-1, N_EXPERTS)
            new_x.append(_moe_call(hf, comb, x1, modg, lw, g_fin, final=final))
            outs.append((hlast.reshape(-1, D), _from_tiled(nlbuf), _from_tiled(ncbuf)))
        xp, xs = new_x

    stack = lambda outs, k: jnp.stack([o[k] for o in outs])
    return (_from_tiled(xp), _from_tiled(xs),
            stack(outs_p, 0), stack(outs_p, 1), stack(outs_p, 2),
            stack(outs_s, 0), stack(outs_s, 1), stack(outs_s, 2))
```

```python
import functools

import jax
import jax.numpy as jnp
from jax import lax
from jax.experimental import pallas as pl
from jax.experimental.pallas import tpu as pltpu

F32 = jnp.float32
BF16 = jnp.bfloat16
I32 = jnp.int32

D = 1024
DEPTH = 2
N_EXPERTS = 16
EXPERTS_PER_GROUP = 4
N_GROUPS = N_EXPERTS // EXPERTS_PER_GROUP
TOP_K = 2
D_EXPERT = D // 2
LRU_TAPS = 4
CONF_TAPS = 31
LRU_C = 8.0
RMS_EPS = 1e-6
LN_EPS = 1e-5
GATE_BLOCK = 256
N_GATE_BLOCKS = D // GATE_BLOCK
MIX_ROWS = 512
MAX_BATCH_TILE = 32
CONV_CHUNK_ROWS = 128
CONV_CHUNK_LANES = 256
MOE_TILE = 512
SEG_ALIGN = 16
SORTED_ROWS = TOP_K * MOE_TILE + N_EXPERTS * SEG_ALIGN
EXPERT_CHUNK = 512
VMEM_LIMIT = 56 * 1024 * 1024


def _sigmoid(x):
    return 1.0 / (1.0 + jnp.exp(-x))


def _silu(x):
    return x * _sigmoid(x)


def _gelu_tanh(x):
    c = 0.7978845608028654
    return 0.5 * x * (1.0 + jnp.tanh(c * (x + 0.044715 * (x * x * x))))


def _const_spec(shape):
    zeros = (0,) * len(shape)
    return pl.BlockSpec(shape, lambda *_: zeros, pipeline_mode=pl.Buffered(1))


def _ada_kernel(c_ref, w_ref, b_ref, o_ref):
    s = _silu(c_ref[...]).astype(BF16)
    o_ref[0] = jnp.dot(s, w_ref[0], preferred_element_type=F32) + b_ref[0]


def _ada_call(c_all, w_ada, b_ada):
    nb = c_all.shape[0]
    tn = 1536
    return pl.pallas_call(
        _ada_kernel,
        out_shape=jax.ShapeDtypeStruct((DEPTH, nb, 6 * D), F32),
        grid=(DEPTH, 6 * D // tn),
        in_specs=[
            pl.BlockSpec((nb, D), lambda l, j: (0, 0)),
            pl.BlockSpec((1, D, tn), lambda l, j: (l, 0, j)),
            pl.BlockSpec((1, 1, tn), lambda l, j: (l, 0, j)),
        ],
        out_specs=pl.BlockSpec((1, nb, tn), lambda l, j: (l, 0, j)),
        name="adaln",
    )(c_all, w_ada, b_ada)


def _depthwise_conv(full_ref, w_ref, out_ref, *, taps, t_tile, batch):
    tc = max(1, CONV_CHUNK_ROWS // batch)
    n_chunks = t_tile // tc

    def chunk(c, carry):
        t0 = c * tc
        for l0 in range(0, D, CONV_CHUNK_LANES):
            lanes = slice(l0, l0 + CONV_CHUNK_LANES)
            acc = jnp.zeros((tc, batch, CONV_CHUNK_LANES), F32)
            for k in range(taps):
                acc = acc + full_ref[pl.ds(t0 + k, tc), :, lanes] * w_ref[k, :, lanes]
            out_ref[pl.ds(t0, tc), :, lanes] = acc
        return carry

    lax.fori_loop(0, n_chunks, chunk, 0)


def _route(logits):
    m = logits[0]
    for e in range(1, N_EXPERTS):
        m = jnp.maximum(m, logits[e])
    ex = [jnp.exp(l - m) for l in logits]
    den = ex[0]
    for e in range(1, N_EXPERTS):
        den = den + ex[e]
    p = [v / den for v in ex]
    sums = []
    for g in range(N_GROUPS):
        q = p[g * EXPERTS_PER_GROUP:(g + 1) * EXPERTS_PER_GROUP]
        m1 = jnp.maximum(jnp.maximum(q[0], q[1]), jnp.maximum(q[2], q[3]))
        m2 = None
        for i in range(EXPERTS_PER_GROUP):
            for j in range(i + 1, EXPERTS_PER_GROUP):
                mn = jnp.minimum(q[i], q[j])
                m2 = mn if m2 is None else jnp.maximum(m2, mn)
        sums.append(m1 + m2)
    best = sums[0]
    grp = jnp.zeros(best.shape, I32)
    for g in range(1, N_GROUPS):
        take = sums[g] > best
        best = jnp.where(take, sums[g], best)
        grp = jnp.where(take, g, grp)
    combine, selected = [], []
    for g in range(N_GROUPS):
        q = p[g * EXPERTS_PER_GROUP:(g + 1) * EXPERTS_PER_GROUP]
        in_grp = grp == g
        for i in range(EXPERTS_PER_GROUP):
            rank = jnp.zeros(best.shape, I32)
            for j in range(EXPERTS_PER_GROUP):
                if j == i:
                    continue
                ahead = (q[j] > q[i]) if j > i else (q[j] >= q[i])
                rank = rank + ahead.astype(I32)
            sel = jnp.where(in_grp, rank, TOP_K) < TOP_K
            combine.append(jnp.where(sel, q[i] / best, 0.0))
            selected.append(jnp.where(sel, 1.0, 0.0))
    return combine, selected


def _mix_kernel(x_ref, mod_ref, h0_ref, lbuf_ref, cbuf_ref,
                gmix_ref, gffn_ref, win_ref, bin_ref,
                lcw_ref, lcb_ref, wa_ref, ba_ref, wx_ref, bx_ref, lam_ref, wlo_ref,
                ccw_ref, ccb_ref, lng_ref, lnb_ref, wco_ref, wout_ref,
                wrh_ref, wrl_ref, br_ref,
                x1_ref, hf_ref, route_ref, count_ref, hlast_ref, nlbuf_ref, ncbuf_ref,
                fu_ref, fg_ref, a_ref, b_ref, hcar_ref,
                *, t_tile, batch, reset):
    j = pl.program_id(1)
    n_t = pl.num_programs(1)
    rows = t_tile * batch

    @pl.when(j == 0)
    def _():
        fu_ref[0:LRU_TAPS - 1] = lbuf_ref[0]
        fg_ref[0:CONF_TAPS - 1] = cbuf_ref[0]
        hcar_ref[...] = h0_ref[0]

    x = x_ref[0]
    sh1, sc1, gt1 = mod_ref[0], mod_ref[1], mod_ref[2]
    sh2, sc2 = mod_ref[3], mod_ref[4]

    ms = jnp.mean(x * x, axis=-1, keepdims=True)
    hm = (x * lax.rsqrt(ms + RMS_EPS) * gmix_ref[...]) * (1.0 + sc1) + sh1
    hm2 = hm.reshape(rows, D).astype(BF16)

    def proj(k):
        cols = slice(k * D, (k + 1) * D)
        return jnp.dot(hm2, win_ref[:, cols], preferred_element_type=F32) + bin_ref[:, cols]

    fu_ref[LRU_TAPS - 1:] = proj(0).reshape(t_tile, batch, D)
    _depthwise_conv(fu_ref, lcw_ref, a_ref, taps=LRU_TAPS, t_tile=t_tile, batch=batch)
    xc = a_ref[...].reshape(rows, D) + lcb_ref[...]
    xcb = xc.astype(BF16)
    r_parts, i_parts = [], []
    for blk in range(N_GATE_BLOCKS):
        cols = slice(blk * GATE_BLOCK, (blk + 1) * GATE_BLOCK)
        r_parts.append(jnp.dot(xcb[:, cols], wa_ref[blk], preferred_element_type=F32))
        i_parts.append(jnp.dot(xcb[:, cols], wx_ref[blk], preferred_element_type=F32))
    r_gate = _sigmoid(jnp.concatenate(r_parts, axis=-1) + ba_ref[...])
    i_gate = _sigmoid(jnp.concatenate(i_parts, axis=-1) + bx_ref[...])
    neg_lam = -lam_ref[...]
    softplus = jnp.maximum(neg_lam, 0.0) + jnp.log1p(jnp.exp(-jnp.abs(neg_lam)))
    log_a = (-LRU_C) * r_gate * softplus
    a = jnp.exp(log_a)
    mult = jnp.sqrt(-jnp.tanh(log_a) * (a * a + 1.0))
    a = a.reshape(t_tile, batch, D)
    mult = mult.reshape(t_tile, batch, D)
    if reset:
        t_idx = lax.broadcasted_iota(I32, (t_tile, batch, D), 0)
        first = jnp.logical_and(t_idx == 0, j == 0)
        a = jnp.where(first, 0.0, a)
        mult = jnp.where(first, 1.0, mult)
    a_ref[...] = a
    b_ref[...] = mult * (i_gate * xc).reshape(t_tile, batch, D)

    def scan_step(t, carry):
        h = a_ref[t] * hcar_ref[...] + b_ref[t]
        hcar_ref[...] = h
        b_ref[t] = h
        return carry

    lax.fori_loop(0, t_tile, scan_step, 0)
    hseq = b_ref[...].reshape(rows, D)
    out_a = jnp.dot((hseq * _gelu_tanh(proj(1))).astype(BF16), wlo_ref[...],
                    preferred_element_type=F32)

    glu = proj(2) * _sigmoid(proj(3))
    fg_ref[CONF_TAPS - 1:] = glu.reshape(t_tile, batch, D)
    _depthwise_conv(fg_ref, ccw_ref, a_ref, taps=CONF_TAPS, t_tile=t_tile, batch=batch)
    dc = a_ref[...].reshape(rows, D) + ccb_ref[...]
    mu = jnp.mean(dc, axis=-1, keepdims=True)
    dcc = dc - mu
    var = jnp.mean(dcc * dcc, axis=-1, keepdims=True)
    ln = dcc * lax.rsqrt(var + LN_EPS) * lng_ref[...] + lnb_ref[...]
    out_b = jnp.dot(_silu(ln).astype(BF16), wco_ref[...], preferred_element_type=F32)

    merged = _sigmoid(proj(4)) * out_a + _sigmoid(proj(5)) * out_b
    mo = jnp.dot(merged.astype(BF16), wout_ref[...], preferred_element_type=F32)
    x1 = x + gt1 * mo.reshape(t_tile, batch, D)
    x1_ref[0] = x1

    ms2 = jnp.mean(x1 * x1, axis=-1, keepdims=True)
    hf = ((x1 * lax.rsqrt(ms2 + RMS_EPS) * gffn_ref[...]) * (1.0 + sc2) + sh2).reshape(rows, D)
    hf_hi = hf.astype(BF16)
    hf_ref[0] = hf_hi
    hf_lo = (hf - hf_hi.astype(F32)).astype(BF16)
    nt = (((1,), (1,)), ((), ()))
    logits = (lax.dot_general(wrh_ref[...], hf_hi, nt, preferred_element_type=F32)
              + lax.dot_general(wrh_ref[...], hf_lo, nt, preferred_element_type=F32)
              + lax.dot_general(wrl_ref[...], hf_hi, nt, preferred_element_type=F32)
              + br_ref[...])
    combine, selected = _route([logits[e:e + 1, :] for e in range(N_EXPERTS)])
    route_ref[0, 0] = jnp.concatenate(combine + selected, axis=0)
    n_sel = jnp.sum(jnp.concatenate(selected, axis=0), axis=-1, keepdims=True)
    count_ref[0, 0] = jnp.broadcast_to(n_sel, (N_EXPERTS, 128))

    fu_ref[0:LRU_TAPS - 1] = fu_ref[t_tile:t_tile + LRU_TAPS - 1]
    fg_ref[0:CONF_TAPS - 1] = fg_ref[t_tile:t_tile + CONF_TAPS - 1]

    @pl.when(j == n_t - 1)
    def _():
        hlast_ref[0] = hcar_ref[...]
        nlbuf_ref[0] = fu_ref[0:LRU_TAPS - 1]
        ncbuf_ref[0] = fg_ref[0:CONF_TAPS - 1]


def _mix_call(x, mod, h0, lbuf, cbuf, lw, *, reset):
    nb, t_len, bt, _ = x.shape
    t_tile = min(t_len, MIX_ROWS // bt)
    n_t = t_len // t_tile
    rows = t_tile * bt
    kern = functools.partial(_mix_kernel, t_tile=t_tile, batch=bt, reset=reset)
    consts = [lw["g_mix"], lw["g_ffn"], lw["w_in"], lw["b_in"],
              lw["lru_conv_w"], lw["lru_conv_b"], lw["wa"], lw["lru_ba"], lw["wx"], lw["lru_bx"],
              lw["lru_lambda"], lw["w_lru_o"], lw["conf_conv_w"], lw["conf_conv_b"],
              lw["conf_ln_g"], lw["conf_ln_b"], lw["w_conf_o"], lw["w_out"],
              lw["wr_hi"], lw["wr_lo"], lw["b_router"]]
    in_specs = [
        pl.BlockSpec((1, t_tile, bt, D), lambda b, j: (b, j, 0, 0)),
        pl.BlockSpec((6, bt, D), lambda b, j: (0, b, 0)),
        pl.BlockSpec((1, bt, D), lambda b, j: (b, 0, 0)),
        pl.BlockSpec((1, LRU_TAPS - 1, bt, D), lambda b, j: (b, 0, 0, 0)),
        pl.BlockSpec((1, CONF_TAPS - 1, bt, D), lambda b, j: (b, 0, 0, 0)),
    ]
    in_specs += [_const_spec(c.shape) for c in consts]
    out_shape = (
        jax.ShapeDtypeStruct((nb, t_len, bt, D), F32),
        jax.ShapeDtypeStruct((nb * n_t, rows, D), BF16),
        jax.ShapeDtypeStruct((nb, n_t, 2 * N_EXPERTS, rows), F32),
        jax.ShapeDtypeStruct((nb, n_t, N_EXPERTS, 128), F32),
        jax.ShapeDtypeStruct((nb, bt, D), F32),
        jax.ShapeDtypeStruct((nb, LRU_TAPS - 1, bt, D), F32),
        jax.ShapeDtypeStruct((nb, CONF_TAPS - 1, bt, D), F32),
    )
    out_specs = (
        pl.BlockSpec((1, t_tile, bt, D), lambda b, j: (b, j, 0, 0)),
        pl.BlockSpec((1, rows, D), lambda b, j: (b * n_t + j, 0, 0)),
        pl.BlockSpec((1, 1, 2 * N_EXPERTS, rows), lambda b, j: (b, j, 0, 0)),
        pl.BlockSpec((1, 1, N_EXPERTS, 128), lambda b, j: (b, j, 0, 0)),
        pl.BlockSpec((1, bt, D), lambda b, j: (b, 0, 0)),
        pl.BlockSpec((1, LRU_TAPS - 1, bt, D), lambda b, j: (b, 0, 0, 0)),
        pl.BlockSpec((1, CONF_TAPS - 1, bt, D), lambda b, j: (b, 0, 0, 0)),
    )
    scratch = [
        pltpu.VMEM((t_tile + LRU_TAPS - 1, bt, D), F32),
        pltpu.VMEM((t_tile + CONF_TAPS - 1, bt, D), F32),
        pltpu.VMEM((t_tile, bt, D), F32),
        pltpu.VMEM((t_tile, bt, D), F32),
        pltpu.VMEM((bt, D), F32),
    ]
    return pl.pallas_call(
        kern,
        out_shape=out_shape,
        grid=(nb, n_t),
        in_specs=in_specs,
        out_specs=out_specs,
        scratch_shapes=scratch,
        compiler_params=pltpu.CompilerParams(
            dimension_semantics=("arbitrary", "arbitrary"), vmem_limit_bytes=VMEM_LIMIT),
        name="mix_reset" if reset else "mix_cont",
    )(x, mod, h0, lbuf, cbuf, *consts)


def _segment_dmas(seg_len_ref, seg_start_ref, tile, make_copy):
    offs, off = [], 0
    for e in range(N_EXPERTS):
        offs.append(off)
        off = off + seg_len_ref[tile * N_EXPERTS + e]
    for e in range(N_EXPERTS):
        n_blocks = seg_len_ref[tile * N_EXPERTS + e] // SEG_ALIGN
        start = seg_start_ref[tile * N_EXPERTS + e]

        def issue(k, carry, e=e, start=start):
            local = pl.multiple_of(offs[e] + k * SEG_ALIGN, SEG_ALIGN)
            packed = pl.multiple_of(start + k * SEG_ALIGN, SEG_ALIGN)
            make_copy(e, local, packed).start()
            return carry

        lax.fori_loop(0, n_blocks, issue, 0)

    def wait(k, carry):
        make_copy(0, 0, 0).wait()
        return carry

    lax.fori_loop(0, off // SEG_ALIGN, wait, 0)


def _dispatch_kernel(seg_len_ref, seg_start_ref, total_ref,
                     hf_ref, route_ref, xbuf_ref, meta_ref, xs_ref, zero_ref, sem):
    i = pl.program_id(0)
    comb = route_ref[0, 0:N_EXPERTS, :]
    sel = route_ref[0, N_EXPERTS:2 * N_EXPERTS, :]
    on = sel > 0.0

    r_idx = lax.broadcasted_iota(I32, (MOE_TILE, MOE_TILE), 0)
    c_idx = lax.broadcasted_iota(I32, (MOE_TILE, MOE_TILE), 1)
    upper = jnp.where(r_idx <= c_idx, 1.0, 0.0).astype(BF16)
    csum = jnp.dot(sel.astype(BF16), upper, preferred_element_type=F32)
    rank = csum - sel
    count = csum[:, MOE_TILE - 1:MOE_TILE].astype(I32)
    padded = ((count + (SEG_ALIGN - 1)) // SEG_ALIGN) * SEG_ALIGN
    e_r = lax.broadcasted_iota(I32, (N_EXPERTS, N_EXPERTS), 0)
    e_c = lax.broadcasted_iota(I32, (N_EXPERTS, N_EXPERTS), 1)
    lower = jnp.where(e_c < e_r, 1.0, 0.0).astype(BF16)
    padded_b = jnp.broadcast_to(padded.astype(F32), (N_EXPERTS, 128)).astype(BF16)
    seg_off = jnp.dot(lower, padded_b, preferred_element_type=F32)[:, 0:1]

    pos = seg_off + rank
    p_lo = jnp.min(jnp.where(on, pos, 4.0 * SORTED_ROWS), axis=0, keepdims=True)
    p_hi = jnp.max(jnp.where(on, pos, -1.0), axis=0, keepdims=True)
    w_lo = jnp.sum(jnp.where(jnp.logical_and(on, pos == p_lo), comb, 0.0), axis=0, keepdims=True)
    w_hi = jnp.sum(jnp.where(jnp.logical_and(on, pos == p_hi), comb, 0.0), axis=0, keepdims=True)
    meta_ref[0] = jnp.concatenate(
        [p_lo, p_hi, w_lo, w_hi, jnp.zeros((4, MOE_TILE), F32)], axis=0)

    s_idx = lax.broadcasted_iota(I32, (SORTED_ROWS, MOE_TILE), 0)
    one_hot = jnp.where(s_idx == p_lo.astype(I32), 1.0,
                        jnp.where(s_idx == p_hi.astype(I32), 1.0, 0.0)).astype(BF16)
    xs_ref[...] = jnp.dot(one_hot, hf_ref[...], preferred_element_type=F32).astype(BF16)

    def seg_copy(e, local, packed):
        return pltpu.make_async_copy(xs_ref.at[pl.ds(local, SEG_ALIGN), :],
                                     xbuf_ref.at[e, pl.ds(packed, SEG_ALIGN), :], sem.at[0])

    _segment_dmas(seg_len_ref, seg_start_ref, i, seg_copy)

    @pl.when(i == pl.num_programs(0) - 1)
    def _():
        zero_ref[...] = jnp.zeros_like(zero_ref)
        copies = []
        for e in range(N_EXPERTS):
            row = pl.multiple_of(total_ref[e], SEG_ALIGN)
            copies.append(pltpu.make_async_copy(
                zero_ref, xbuf_ref.at[e, pl.ds(row, EXPERT_CHUNK), :], sem.at[1]))
            copies[-1].start()
        for c in copies:
            c.wait()


def _dispatch_call(hf, route, seg_len, seg_start, total, cap):
    n_tiles = route.shape[0]
    grid_spec = pltpu.PrefetchScalarGridSpec(
        num_scalar_prefetch=3,
        grid=(n_tiles,),
        in_specs=[
            pl.BlockSpec((MOE_TILE, D), lambda i, *_: (i, 0)),
            pl.BlockSpec((1, 2 * N_EXPERTS, MOE_TILE), lambda i, *_: (i, 0, 0)),
        ],
        out_specs=(
            pl.BlockSpec(memory_space=pl.ANY),
            pl.BlockSpec((1, 8, MOE_TILE), lambda i, *_: (i, 0, 0)),
        ),
        scratch_shapes=[
            pltpu.VMEM((SORTED_ROWS, D), BF16),
            pltpu.VMEM((EXPERT_CHUNK, D), BF16),
            pltpu.SemaphoreType.DMA((2,)),
        ],
    )
    return pl.pallas_call(
        _dispatch_kernel,
        out_shape=(jax.ShapeDtypeStruct((N_EXPERTS, cap, D), BF16),
                   jax.ShapeDtypeStruct((n_tiles, 8, MOE_TILE), F32)),
        grid_spec=grid_spec,
        compiler_params=pltpu.CompilerParams(
            dimension_semantics=("arbitrary",), vmem_limit_bytes=VMEM_LIMIT),
        name="moe_dispatch",
    )(seg_len, seg_start, total, hf, route)


def _expert_kernel(e_ref, c_ref, valid_ref, x_ref, wg_ref, wu_ref, wd_ref, y_ref):
    s = pl.program_id(0)

    @pl.when(valid_ref[s] == 1)
    def _():
        x = x_ref[0]
        g = jnp.dot(x, wg_ref[0], preferred_element_type=F32)
        u = jnp.dot(x, wu_ref[0], preferred_element_type=F32)
        y = jnp.dot((_silu(g) * u).astype(BF16), wd_ref[0], preferred_element_type=F32)
        y_ref[0] = y.astype(BF16)


def _expert_call(xbuf, e_tab, c_tab, valid_tab, lw):
    cap = xbuf.shape[1]
    n_steps = e_tab.shape[0]
    grid_spec = pltpu.PrefetchScalarGridSpec(
        num_scalar_prefetch=3,
        grid=(n_steps,),
        in_specs=[
            pl.BlockSpec((1, EXPERT_CHUNK, D), lambda s, e, c, v: (e[s], c[s], 0)),
            pl.BlockSpec((1, D, D_EXPERT), lambda s, e, c, v: (e[s], 0, 0)),
            pl.BlockSpec((1, D, D_EXPERT), lambda s, e, c, v: (e[s], 0, 0)),
            pl.BlockSpec((1, D_EXPERT, D), lambda s, e, c, v: (e[s], 0, 0)),
        ],
        out_specs=pl.BlockSpec((1, EXPERT_CHUNK, D), lambda s, e, c, v: (e[s], c[s], 0)),
    )
    return pl.pallas_call(
        _expert_kernel,
        out_shape=jax.ShapeDtypeStruct((N_EXPERTS, cap, D), BF16),
        grid_spec=grid_spec,
        compiler_params=pltpu.CompilerParams(
            dimension_semantics=("arbitrary",), vmem_limit_bytes=VMEM_LIMIT),
        name="moe_experts",
    )(e_tab, c_tab, valid_tab, xbuf, lw["w_gate"], lw["w_up"], lw["w_down"])


def _combine_kernel(seg_len_ref, seg_start_ref, ybuf_ref, meta_ref, x1_ref, mod_ref, gfin_ref,
                    o_ref, ys_ref, sem, *, final):
    i = pl.program_id(0)

    @pl.when(i == 0)
    def _():
        ys_ref[...] = jnp.zeros_like(ys_ref)

    def seg_copy(e, local, packed):
        return pltpu.make_async_copy(ybuf_ref.at[e, pl.ds(packed, SEG_ALIGN), :],
                                     ys_ref.at[pl.ds(local, SEG_ALIGN), :], sem.at[0])

    _segment_dmas(seg_len_ref, seg_start_ref, i, seg_copy)

    meta = meta_ref[0]
    p_lo = meta[:, 0:1].astype(I32)
    p_hi = meta[:, 1:2].astype(I32)
    w_lo = meta[:, 2:3]
    w_hi = meta[:, 3:4]
    s_idx = lax.broadcasted_iota(I32, (MOE_TILE, SORTED_ROWS), 1)
    ys = ys_ref[...]
    y_lo = jnp.dot(jnp.where(s_idx == p_lo, 1.0, 0.0).astype(BF16), ys, preferred_element_type=F32)
    y_hi = jnp.dot(jnp.where(s_idx == p_hi, 1.0, 0.0).astype(BF16), ys, preferred_element_type=F32)
    moe = w_lo * y_lo + w_hi * y_hi

    nbt, t_tile, bt, _ = x1_ref.shape
    gt2 = mod_ref[5].reshape(nbt, 1, bt, D)
    x2 = x1_ref[...] + gt2 * moe.reshape(nbt, t_tile, bt, D)
    if final:
        ms = jnp.mean(x2 * x2, axis=-1, keepdims=True)
        x2 = x2 * lax.rsqrt(ms + RMS_EPS) * gfin_ref[...]
    o_ref[...] = x2


def _combine_call(ybuf, meta_t, x1, mod, g_final, seg_len, seg_start, *, final):
    nb, t_len, bt, _ = x1.shape
    if nb * t_len * bt == MOE_TILE:
        nbt, t_tile = nb, t_len
    else:
        nbt, t_tile = 1, MOE_TILE // bt
    n_t = t_len // t_tile
    grid_spec = pltpu.PrefetchScalarGridSpec(
        num_scalar_prefetch=2,
        grid=((nb // nbt) * n_t,),
        in_specs=[
            pl.BlockSpec(memory_space=pl.ANY),
            pl.BlockSpec((1, MOE_TILE, 8), lambda i, *_: (i, 0, 0)),
            pl.BlockSpec((nbt, t_tile, bt, D), lambda i, *_: (i // n_t, i % n_t, 0, 0)),
            pl.BlockSpec((6, nbt * bt, D), lambda i, *_: (0, i // n_t, 0)),
            pl.BlockSpec((1, D), lambda i, *_: (0, 0)),
        ],
        out_specs=pl.BlockSpec((nbt, t_tile, bt, D), lambda i, *_: (i // n_t, i % n_t, 0, 0)),
        scratch_shapes=[
            pltpu.VMEM((SORTED_ROWS, D), BF16),
            pltpu.SemaphoreType.DMA((1,)),
        ],
    )
    return pl.pallas_call(
        functools.partial(_combine_kernel, final=final),
        out_shape=jax.ShapeDtypeStruct((nb, t_len, bt, D), F32),
        grid_spec=grid_spec,
        compiler_params=pltpu.CompilerParams(
            dimension_semantics=("arbitrary",), vmem_limit_bytes=VMEM_LIMIT),
        name="moe_combine_final" if final else "moe_combine",
    )(seg_len, seg_start, ybuf, meta_t, x1, mod, g_final)


def _moe_sparse(hf, route, count, x1, mod, lw, g_final, *, final):
    n_tok = hf.shape[0]
    n_tiles = n_tok // MOE_TILE
    route = route.transpose(0, 1, 3, 2).reshape(n_tiles, MOE_TILE, 2 * N_EXPERTS).transpose(0, 2, 1)

    count = count[..., 0].astype(I32).reshape(n_tiles, -1, N_EXPERTS).sum(axis=1)
    seg_len = (count + SEG_ALIGN - 1) // SEG_ALIGN * SEG_ALIGN
    seg_start = jnp.cumsum(seg_len, axis=0) - seg_len
    total = jnp.sum(seg_len, axis=0)
    max_rows = n_tok + SEG_ALIGN * n_tiles
    cap = -(-max_rows // EXPERT_CHUNK) * EXPERT_CHUNK + EXPERT_CHUNK
    n_chunks = (total + EXPERT_CHUNK - 1) // EXPERT_CHUNK
    chunk_end = jnp.cumsum(n_chunks)
    max_steps = -(-(TOP_K * n_tok + N_EXPERTS * SEG_ALIGN * n_tiles) // EXPERT_CHUNK) + N_EXPERTS
    step = jnp.arange(max_steps, dtype=I32)
    valid = step < chunk_end[-1]
    step = jnp.minimum(step, chunk_end[-1] - 1)
    e_tab = jnp.minimum(jnp.searchsorted(chunk_end, step, side="right"), N_EXPERTS - 1).astype(I32)
    c_tab = (step - (chunk_end - n_chunks)[e_tab]).astype(I32)

    seg_len_flat = seg_len.reshape(-1)
    seg_start_flat = seg_start.reshape(-1).astype(I32)
    xbuf, meta = _dispatch_call(hf, route, seg_len_flat, seg_start_flat, total.astype(I32), cap)
    ybuf = _expert_call(xbuf, e_tab, c_tab, valid.astype(I32), lw)
    meta_t = meta.transpose(0, 2, 1)
    return _combine_call(ybuf, meta_t, x1, mod, g_final, seg_len_flat, seg_start_flat, final=final)


def _block_diag(w):
    heads_per_block = GATE_BLOCK // w.shape[-1]
    w = w.reshape(N_GATE_BLOCKS, heads_per_block, w.shape[-2], w.shape[-1])
    eye = jnp.eye(heads_per_block, dtype=w.dtype)
    out = jnp.einsum("ghij,hk->ghikj", w, eye)
    return out.reshape(N_GATE_BLOCKS, GATE_BLOCK, GATE_BLOCK).astype(BF16)


def _to_tiled(x, bt):
    b, t, d = x.shape
    return x.reshape(b // bt, bt, t, d).transpose(0, 2, 1, 3)


def _from_tiled(x):
    nb, t, bt, d = x.shape
    return x.transpose(0, 2, 1, 3).reshape(nb * bt, t, d)


def kernel(x_prompt, x_sample, state_lru_h, state_lru_conv, state_conf_conv, c_prompt, c_sample, w_ada, b_ada, g_mix, g_ffn, w_in, b_in, lru_conv_w, lru_conv_b, lru_wa, lru_ba, lru_wx, lru_bx, lru_lambda, w_lru_o, conf_conv_w, conf_conv_b, conf_ln_g, conf_ln_b, w_conf_o, w_out, w_router, b_router, w_gate, w_up, w_down, g_final):
    bp = x_prompt.shape[0]
    bs = x_sample.shape[0]
    btp = min(bp, MAX_BATCH_TILE)
    bts = min(bs, MAX_BATCH_TILE)
    row = lambda v: v.reshape(1, -1)

    c_all = jnp.concatenate([c_prompt, c_sample], axis=0)
    mod = _ada_call(c_all, w_ada.astype(BF16), b_ada.reshape(DEPTH, 1, 6 * D))
    mod = mod.reshape(DEPTH, bp + bs, 6, D).transpose(0, 2, 1, 3)

    wr_t = w_router.T
    wr_hi = wr_t.astype(BF16)
    wr_lo = (wr_t - wr_hi.astype(F32)).astype(BF16)
    g_fin = row(g_final)

    xp = _to_tiled(x_prompt, btp)
    xs = _to_tiled(x_sample, bts)

    outs_p, outs_s = [], []
    for l in range(DEPTH):
        lw = {
            "g_mix": row(g_mix[l]), "g_ffn": row(g_ffn[l]),
            "w_in": w_in[l].astype(BF16), "b_in": row(b_in[l]),
            "lru_conv_w": lru_conv_w[l][:, None, :], "lru_conv_b": row(lru_conv_b[l]),
            "wa": _block_diag(lru_wa[l]), "lru_ba": row(lru_ba[l]),
            "wx": _block_diag(lru_wx[l]), "lru_bx": row(lru_bx[l]),
            "lru_lambda": row(lru_lambda[l]), "w_lru_o": w_lru_o[l].astype(BF16),
            "conf_conv_w": conf_conv_w[l][:, None, :], "conf_conv_b": row(conf_conv_b[l]),
            "conf_ln_g": row(conf_ln_g[l]), "conf_ln_b": row(conf_ln_b[l]),
            "w_conf_o": w_conf_o[l].astype(BF16), "w_out": w_out[l].astype(BF16),
            "wr_hi": wr_hi, "wr_lo": wr_lo, "b_router": b_router.reshape(N_EXPERTS, 1),
            "w_gate": w_gate[l].astype(BF16), "w_up": w_up[l].astype(BF16),
            "w_down": w_down[l].astype(BF16),
        }
        final = l == DEPTH - 1
        groups = (
            (xp, mod[l, :, :bp], jnp.zeros((bp // btp, btp, D), F32),
             jnp.zeros((bp // btp, LRU_TAPS - 1, btp, D), F32),
             jnp.zeros((bp // btp, CONF_TAPS - 1, btp, D), F32), True),
            (xs, mod[l, :, bp:], state_lru_h[l].reshape(bs // bts, bts, D),
             _to_tiled(state_lru_conv[l], bts), _to_tiled(state_conf_conv[l], bts), False),
        )
        new_x = []
        for (xg, modg, h0, lbuf, cbuf, reset), outs in zip(groups, (outs_p, outs_s)):
            x1, hf, route, count, hlast, nlbuf, ncbuf = _mix_call(
                xg, modg, h0, lbuf, cbuf, lw, reset=reset)
            new_x.append(_moe_sparse(hf.reshape(-1, D), route, count, x1, modg, lw, g_fin, final=final))
            outs.append((hlast.reshape(-1, D), _from_tiled(nlbuf), _from_tiled(ncbuf)))
        xp, xs = new_x

    stack = lambda outs, k: jnp.stack([o[k] for o in outs])
    return (_from_tiled(xp), _from_tiled(xs),
            stack(outs_p, 0), stack(outs_p, 1), stack(outs_p, 2),
            stack(outs_s, 0), stack(outs_s, 1), stack(outs_s, 2))
```

```python
import functools

import jax
import jax.numpy as jnp
from jax import lax
from jax.experimental import pallas as pl
from jax.experimental.pallas import tpu as pltpu

F32 = jnp.float32
BF16 = jnp.bfloat16
I32 = jnp.int32

D = 1024
DEPTH = 2
N_EXPERTS = 16
EXPERTS_PER_GROUP = 4
N_GROUPS = N_EXPERTS // EXPERTS_PER_GROUP
TOP_K = 2
D_EXPERT = D // 2
LRU_TAPS = 4
CONF_TAPS = 31
LRU_C = 8.0
RMS_EPS = 1e-6
LN_EPS = 1e-5
GATE_BLOCK = 256
N_GATE_BLOCKS = D // GATE_BLOCK
MIX_ROWS = 512
MAX_BATCH_TILE = 32
CONV_CHUNK_ROWS = 128
CONV_CHUNK_LANES = 128
MOE_TILE = 512
SEG_ALIGN = 16
SORTED_ROWS = TOP_K * MOE_TILE + N_EXPERTS * SEG_ALIGN
EXPERT_CHUNK = 512
SMALL_EXPERT_CHUNK = 128
VMEM_LIMIT = 56 * 1024 * 1024


def _sigmoid(x):
    return 1.0 / (1.0 + jnp.exp(-x))


def _silu(x):
    return x * _sigmoid(x)


def _gelu_tanh(x):
    c = 0.7978845608028654
    return 0.5 * x * (1.0 + jnp.tanh(c * (x + 0.044715 * (x * x * x))))


def _const_spec(shape):
    zeros = (0,) * len(shape)
    return pl.BlockSpec(shape, lambda *_: zeros, pipeline_mode=pl.Buffered(1))


def _ada_kernel(c_ref, w_ref, b_ref, o_ref):
    s = _silu(c_ref[...]).astype(BF16)
    o_ref[0] = jnp.dot(s, w_ref[0], preferred_element_type=F32) + b_ref[0]


def _ada_call(c_all, w_ada, b_ada):
    nb = c_all.shape[0]
    tn = 1536
    return pl.pallas_call(
        _ada_kernel,
        out_shape=jax.ShapeDtypeStruct((DEPTH, nb, 6 * D), F32),
        grid=(DEPTH, 6 * D // tn),
        in_specs=[
            pl.BlockSpec((nb, D), lambda l, j: (0, 0)),
            pl.BlockSpec((1, D, tn), lambda l, j: (l, 0, j)),
            pl.BlockSpec((1, 1, tn), lambda l, j: (l, 0, j)),
        ],
        out_specs=pl.BlockSpec((1, nb, tn), lambda l, j: (l, 0, j)),
        name="adaln",
    )(c_all, w_ada, b_ada)


def _depthwise_conv(full_ref, w_ref, out_ref, *, taps, t_tile, batch):
    tc = max(1, CONV_CHUNK_ROWS // batch)
    n_chunks = t_tile // tc
    groups = tc * batch // 8
    for l0 in range(0, D, CONV_CHUNK_LANES):
        lanes = slice(l0, l0 + CONV_CHUNK_LANES)
        w = [w_ref[k, :, lanes] for k in range(taps)]

        def chunk(c, carry, lanes=lanes, w=w):
            t0 = c * tc
            acc = jnp.zeros((groups, 8, CONV_CHUNK_LANES), F32)
            for k in range(taps):
                rows = full_ref[pl.ds(t0 + k, tc), :, lanes].reshape(groups, 8, CONV_CHUNK_LANES)
                acc = acc + rows * w[k]
            out_ref[pl.ds(t0, tc), :, lanes] = acc.reshape(tc, batch, CONV_CHUNK_LANES)
            return carry

        lax.fori_loop(0, n_chunks, chunk, 0)


def _route(logits):
    m = logits[0]
    for e in range(1, N_EXPERTS):
        m = jnp.maximum(m, logits[e])
    ex = [jnp.exp(l - m) for l in logits]
    den = ex[0]
    for e in range(1, N_EXPERTS):
        den = den + ex[e]
    p = [v / den for v in ex]
    sums = []
    for g in range(N_GROUPS):
        q = p[g * EXPERTS_PER_GROUP:(g + 1) * EXPERTS_PER_GROUP]
        m1 = jnp.maximum(jnp.maximum(q[0], q[1]), jnp.maximum(q[2], q[3]))
        m2 = None
        for i in range(EXPERTS_PER_GROUP):
            for j in range(i + 1, EXPERTS_PER_GROUP):
                mn = jnp.minimum(q[i], q[j])
                m2 = mn if m2 is None else jnp.maximum(m2, mn)
        sums.append(m1 + m2)
    best = sums[0]
    grp = jnp.zeros(best.shape, I32)
    for g in range(1, N_GROUPS):
        take = sums[g] > best
        best = jnp.where(take, sums[g], best)
        grp = jnp.where(take, g, grp)
    combine, selected = [], []
    for g in range(N_GROUPS):
        q = p[g * EXPERTS_PER_GROUP:(g + 1) * EXPERTS_PER_GROUP]
        in_grp = grp == g
        for i in range(EXPERTS_PER_GROUP):
            rank = jnp.zeros(best.shape, I32)
            for j in range(EXPERTS_PER_GROUP):
                if j == i:
                    continue
                ahead = (q[j] > q[i]) if j > i else (q[j] >= q[i])
                rank = rank + ahead.astype(I32)
            sel = jnp.where(in_grp, rank, TOP_K) < TOP_K
            combine.append(jnp.where(sel, q[i] / best, 0.0))
            selected.append(jnp.where(sel, 1.0, 0.0))
    return combine, selected


def _mix_kernel(x_ref, mod_ref, h0_ref, lbuf_ref, cbuf_ref,
                gmix_ref, gffn_ref, win_ref, bin_ref,
                lcw_ref, lcb_ref, wa_ref, ba_ref, wx_ref, bx_ref, lam_ref, wlo_ref,
                ccw_ref, ccb_ref, lng_ref, lnb_ref, wco_ref, wout_ref,
                wrh_ref, wrl_ref, br_ref,
                x1_ref, hf_ref, route_ref, count_ref, hlast_ref, nlbuf_ref, ncbuf_ref,
                fu_ref, fg_ref, a_ref, b_ref, hcar_ref,
                *, t_tile, batch, reset):
    j = pl.program_id(1)
    n_t = pl.num_programs(1)
    rows = t_tile * batch

    @pl.when(j == 0)
    def _():
        fu_ref[0:LRU_TAPS - 1] = lbuf_ref[0]
        fg_ref[0:CONF_TAPS - 1] = cbuf_ref[0]
        hcar_ref[...] = h0_ref[0]

    x = x_ref[0]
    sh1, sc1, gt1 = mod_ref[0], mod_ref[1], mod_ref[2]
    sh2, sc2 = mod_ref[3], mod_ref[4]

    ms = jnp.mean(x * x, axis=-1, keepdims=True)
    hm = (x * lax.rsqrt(ms + RMS_EPS) * gmix_ref[...]) * (1.0 + sc1) + sh1
    hm2 = hm.reshape(rows, D).astype(BF16)

    def proj(k):
        cols = slice(k * D, (k + 1) * D)
        return jnp.dot(hm2, win_ref[:, cols], preferred_element_type=F32) + bin_ref[:, cols]

    fu_ref[LRU_TAPS - 1:] = proj(0).reshape(t_tile, batch, D)
    _depthwise_conv(fu_ref, lcw_ref, a_ref, taps=LRU_TAPS, t_tile=t_tile, batch=batch)
    xc = a_ref[...].reshape(rows, D) + lcb_ref[...]
    xcb = xc.astype(BF16)
    r_parts, i_parts = [], []
    for blk in range(N_GATE_BLOCKS):
        cols = slice(blk * GATE_BLOCK, (blk + 1) * GATE_BLOCK)
        r_parts.append(jnp.dot(xcb[:, cols], wa_ref[blk], preferred_element_type=F32))
        i_parts.append(jnp.dot(xcb[:, cols], wx_ref[blk], preferred_element_type=F32))
    r_gate = _sigmoid(jnp.concatenate(r_parts, axis=-1) + ba_ref[...])
    i_gate = _sigmoid(jnp.concatenate(i_parts, axis=-1) + bx_ref[...])
    neg_lam = -lam_ref[...]
    softplus = jnp.maximum(neg_lam, 0.0) + jnp.log1p(jnp.exp(-jnp.abs(neg_lam)))
    log_a = (-LRU_C) * r_gate * softplus
    a = jnp.exp(log_a)
    mult = jnp.sqrt(-jnp.tanh(log_a) * (a * a + 1.0))
    a = a.reshape(t_tile, batch, D)
    mult = mult.reshape(t_tile, batch, D)
    if reset:
        t_idx = lax.broadcasted_iota(I32, (t_tile, batch, D), 0)
        first = jnp.logical_and(t_idx == 0, j == 0)
        a = jnp.where(first, 0.0, a)
        mult = jnp.where(first, 1.0, mult)
    a_ref[...] = a
    b_ref[...] = mult * (i_gate * xc).reshape(t_tile, batch, D)

    def scan_step(t, carry):
        h = a_ref[t] * hcar_ref[...] + b_ref[t]
        hcar_ref[...] = h
        b_ref[t] = h
        return carry

    lax.fori_loop(0, t_tile, scan_step, 0)
    hseq = b_ref[...].reshape(rows, D)
    out_a = jnp.dot((hseq * _gelu_tanh(proj(1))).astype(BF16), wlo_ref[...],
                    preferred_element_type=F32)

    glu = proj(2) * _sigmoid(proj(3))
    fg_ref[CONF_TAPS - 1:] = glu.reshape(t_tile, batch, D)
    _depthwise_conv(fg_ref, ccw_ref, a_ref, taps=CONF_TAPS, t_tile=t_tile, batch=batch)
    dc = a_ref[...].reshape(rows, D) + ccb_ref[...]
    mu = jnp.mean(dc, axis=-1, keepdims=True)
    dcc = dc - mu
    var = jnp.mean(dcc * dcc, axis=-1, keepdims=True)
    ln = dcc * lax.rsqrt(var + LN_EPS) * lng_ref[...] + lnb_ref[...]
    out_b = jnp.dot(_silu(ln).astype(BF16), wco_ref[...], preferred_element_type=F32)

    merged = _sigmoid(proj(4)) * out_a + _sigmoid(proj(5)) * out_b
    mo = jnp.dot(merged.astype(BF16), wout_ref[...], preferred_element_type=F32)
    x1 = x + gt1 * mo.reshape(t_tile, batch, D)
    x1_ref[0] = x1

    ms2 = jnp.mean(x1 * x1, axis=-1, keepdims=True)
    hf = ((x1 * lax.rsqrt(ms2 + RMS_EPS) * gffn_ref[...]) * (1.0 + sc2) + sh2).reshape(rows, D)
    hf_hi = hf.astype(BF16)
    hf_ref[0] = hf_hi
    hf_lo = (hf - hf_hi.astype(F32)).astype(BF16)
    nt = (((1,), (1,)), ((), ()))
    logits = (lax.dot_general(wrh_ref[...], hf_hi, nt, preferred_element_type=F32)
              + lax.dot_general(wrh_ref[...], hf_lo, nt, preferred_element_type=F32)
              + lax.dot_general(wrl_ref[...], hf_hi, nt, preferred_element_type=F32)
              + br_ref[...])
    combine, selected = _route([logits[e:e + 1, :] for e in range(N_EXPERTS)])
    route_ref[0, 0] = jnp.concatenate(combine + selected, axis=0)
    n_sel = jnp.sum(jnp.concatenate(selected, axis=0), axis=-1, keepdims=True)
    count_ref[0, 0] = jnp.broadcast_to(n_sel, (N_EXPERTS, 128))

    fu_ref[0:LRU_TAPS - 1] = fu_ref[t_tile:t_tile + LRU_TAPS - 1]
    fg_ref[0:CONF_TAPS - 1] = fg_ref[t_tile:t_tile + CONF_TAPS - 1]

    @pl.when(j == n_t - 1)
    def _():
        hlast_ref[0] = hcar_ref[...]
        nlbuf_ref[0] = fu_ref[0:LRU_TAPS - 1]
        ncbuf_ref[0] = fg_ref[0:CONF_TAPS - 1]


def _mix_call(x, mod, h0, lbuf, cbuf, lw, *, reset):
    nb, t_len, bt, _ = x.shape
    t_tile = min(t_len, MIX_ROWS // bt)
    n_t = t_len // t_tile
    rows = t_tile * bt
    kern = functools.partial(_mix_kernel, t_tile=t_tile, batch=bt, reset=reset)
    consts = [lw["g_mix"], lw["g_ffn"], lw["w_in"], lw["b_in"],
              lw["lru_conv_w"], lw["lru_conv_b"], lw["wa"], lw["lru_ba"], lw["wx"], lw["lru_bx"],
              lw["lru_lambda"], lw["w_lru_o"], lw["conf_conv_w"], lw["conf_conv_b"],
              lw["conf_ln_g"], lw["conf_ln_b"], lw["w_conf_o"], lw["w_out"],
              lw["wr_hi"], lw["wr_lo"], lw["b_router"]]
    in_specs = [
        pl.BlockSpec((1, t_tile, bt, D), lambda b, j: (b, j, 0, 0)),
        pl.BlockSpec((6, bt, D), lambda b, j: (0, b, 0)),
        pl.BlockSpec((1, bt, D), lambda b, j: (b, 0, 0)),
        pl.BlockSpec((1, LRU_TAPS - 1, bt, D), lambda b, j: (b, 0, 0, 0)),
        pl.BlockSpec((1, CONF_TAPS - 1, bt, D), lambda b, j: (b, 0, 0, 0)),
    ]
    in_specs += [_const_spec(c.shape) for c in consts]
    out_shape = (
        jax.ShapeDtypeStruct((nb, t_len, bt, D), F32),
        jax.ShapeDtypeStruct((nb * n_t, rows, D), BF16),
        jax.ShapeDtypeStruct((nb, n_t, 2 * N_EXPERTS, rows), F32),
        jax.ShapeDtypeStruct((nb, n_t, N_EXPERTS, 128), F32),
        jax.ShapeDtypeStruct((nb, bt, D), F32),
        jax.ShapeDtypeStruct((nb, LRU_TAPS - 1, bt, D), F32),
        jax.ShapeDtypeStruct((nb, CONF_TAPS - 1, bt, D), F32),
    )
    out_specs = (
        pl.BlockSpec((1, t_tile, bt, D), lambda b, j: (b, j, 0, 0)),
        pl.BlockSpec((1, rows, D), lambda b, j: (b * n_t + j, 0, 0)),
        pl.BlockSpec((1, 1, 2 * N_EXPERTS, rows), lambda b, j: (b, j, 0, 0)),
        pl.BlockSpec((1, 1, N_EXPERTS, 128), lambda b, j: (b, j, 0, 0)),
        pl.BlockSpec((1, bt, D), lambda b, j: (b, 0, 0)),
        pl.BlockSpec((1, LRU_TAPS - 1, bt, D), lambda b, j: (b, 0, 0, 0)),
        pl.BlockSpec((1, CONF_TAPS - 1, bt, D), lambda b, j: (b, 0, 0, 0)),
    )
    scratch = [
        pltpu.VMEM((t_tile + LRU_TAPS - 1, bt, D), F32),
        pltpu.VMEM((t_tile + CONF_TAPS - 1, bt, D), F32),
        pltpu.VMEM((t_tile, bt, D), F32),
        pltpu.VMEM((t_tile, bt, D), F32),
        pltpu.VMEM((bt, D), F32),
    ]
    return pl.pallas_call(
        kern,
        out_shape=out_shape,
        grid=(nb, n_t),
        in_specs=in_specs,
        out_specs=out_specs,
        scratch_shapes=scratch,
        compiler_params=pltpu.CompilerParams(
            dimension_semantics=("arbitrary", "arbitrary"), vmem_limit_bytes=VMEM_LIMIT),
        name="mix_reset" if reset else "mix_cont",
    )(x, mod, h0, lbuf, cbuf, *consts)


def _segment_issue(seg_len_ref, seg_start_ref, tile, make_copy):
    off = 0
    for e in range(N_EXPERTS):
        n_blocks = seg_len_ref[tile * N_EXPERTS + e] // SEG_ALIGN
        start = seg_start_ref[tile * N_EXPERTS + e]

        def issue(k, carry, off=off, start=start):
            local = pl.multiple_of(off + k * SEG_ALIGN, SEG_ALIGN)
            packed = pl.multiple_of(start + k * SEG_ALIGN, SEG_ALIGN)
            make_copy(local, packed).start()
            return carry

        lax.fori_loop(0, n_blocks, issue, 0)
        off = off + seg_len_ref[tile * N_EXPERTS + e]


def _segment_wait(seg_len_ref, tile, make_copy):
    rows = 0
    for e in range(N_EXPERTS):
        rows = rows + seg_len_ref[tile * N_EXPERTS + e]

    def wait(k, carry):
        make_copy(0, 0).wait()
        return carry

    lax.fori_loop(0, rows // SEG_ALIGN, wait, 0)


def _dispatch_kernel(seg_len_ref, seg_start_ref, tail_ref,
                     hf_ref, route_ref, xbuf_ref, meta_ref, xs_ref, zero_ref, sem, *, chunk):
    i = pl.program_id(0)
    last = pl.num_programs(0) - 1
    comb = route_ref[0, 0:N_EXPERTS, :]
    sel = route_ref[0, N_EXPERTS:2 * N_EXPERTS, :]
    on = sel > 0.0

    r_idx = lax.broadcasted_iota(I32, (MOE_TILE, MOE_TILE), 0)
    c_idx = lax.broadcasted_iota(I32, (MOE_TILE, MOE_TILE), 1)
    upper = jnp.where(r_idx <= c_idx, 1.0, 0.0).astype(BF16)
    csum = jnp.dot(sel.astype(BF16), upper, preferred_element_type=F32)
    rank = csum - sel
    count = csum[:, MOE_TILE - 1:MOE_TILE].astype(I32)
    padded = ((count + (SEG_ALIGN - 1)) // SEG_ALIGN) * SEG_ALIGN
    e_r = lax.broadcasted_iota(I32, (N_EXPERTS, N_EXPERTS), 0)
    e_c = lax.broadcasted_iota(I32, (N_EXPERTS, N_EXPERTS), 1)
    lower = jnp.where(e_c < e_r, 1.0, 0.0).astype(BF16)
    padded_b = jnp.broadcast_to(padded.astype(F32), (N_EXPERTS, 128)).astype(BF16)
    seg_off = jnp.dot(lower, padded_b, preferred_element_type=F32)[:, 0:1]

    pos = seg_off + rank
    p_lo = jnp.min(jnp.where(on, pos, 4.0 * SORTED_ROWS), axis=0, keepdims=True)
    p_hi = jnp.max(jnp.where(on, pos, -1.0), axis=0, keepdims=True)
    w_lo = jnp.sum(jnp.where(jnp.logical_and(on, pos == p_lo), comb, 0.0), axis=0, keepdims=True)
    w_hi = jnp.sum(jnp.where(jnp.logical_and(on, pos == p_hi), comb, 0.0), axis=0, keepdims=True)
    meta_ref[0] = jnp.concatenate(
        [p_lo, p_hi, w_lo, w_hi, jnp.zeros((4, MOE_TILE), F32)], axis=0)

    s_idx = lax.broadcasted_iota(I32, (SORTED_ROWS, MOE_TILE), 0)
    one_hot = jnp.where(s_idx == p_lo.astype(I32), 1.0,
                        jnp.where(s_idx == p_hi.astype(I32), 1.0, 0.0)).astype(BF16)
    sorted_rows = jnp.dot(one_hot, hf_ref[...], preferred_element_type=F32).astype(BF16)

    def seg_copy(local, packed):
        return pltpu.make_async_copy(xs_ref.at[pl.ds(local, SEG_ALIGN), :],
                                     xbuf_ref.at[pl.ds(packed, SEG_ALIGN), :], sem.at[0])

    @pl.when(i > 0)
    def _():
        _segment_wait(seg_len_ref, i - 1, seg_copy)

    xs_ref[...] = sorted_rows
    _segment_issue(seg_len_ref, seg_start_ref, i, seg_copy)

    @pl.when(i == last)
    def _():
        _segment_wait(seg_len_ref, i, seg_copy)
        zero_ref[...] = jnp.zeros_like(zero_ref)

        def pad_copy(row):
            return pltpu.make_async_copy(zero_ref.at[pl.ds(0, SEG_ALIGN), :],
                                         xbuf_ref.at[pl.ds(row, SEG_ALIGN), :], sem.at[1])

        n_pad = 0
        for e in range(N_EXPERTS):
            def issue(k, carry, e=e):
                pad_copy(pl.multiple_of(tail_ref[e] + k * SEG_ALIGN, SEG_ALIGN)).start()
                return carry

            lax.fori_loop(0, tail_ref[N_EXPERTS + e], issue, 0)
            n_pad = n_pad + tail_ref[N_EXPERTS + e]

        def wait_pad(k, carry):
            pad_copy(0).wait()
            return carry

        lax.fori_loop(0, n_pad, wait_pad, 0)

        def idle_copy(c):
            return pltpu.make_async_copy(
                zero_ref, xbuf_ref.at[pl.ds(pl.multiple_of(c * chunk, chunk), chunk), :], sem.at[1])

        def issue_idle(c, carry):
            idle_copy(c).start()
            return carry

        def wait_idle(c, carry):
            idle_copy(c).wait()
            return carry

        n_chunks = xbuf_ref.shape[0] // chunk
        lax.fori_loop(tail_ref[2 * N_EXPERTS], n_chunks, issue_idle, 0)
        lax.fori_loop(tail_ref[2 * N_EXPERTS], n_chunks, wait_idle, 0)


def _dispatch_call(hf, route, seg_len, seg_start, tail, *, chunk, n_chunks):
    n_tiles = route.shape[0]
    grid_spec = pltpu.PrefetchScalarGridSpec(
        num_scalar_prefetch=3,
        grid=(n_tiles,),
        in_specs=[
            pl.BlockSpec((MOE_TILE, D), lambda i, *_: (i, 0)),
            pl.BlockSpec((1, 2 * N_EXPERTS, MOE_TILE), lambda i, *_: (i, 0, 0)),
        ],
        out_specs=(
            pl.BlockSpec(memory_space=pl.ANY),
            pl.BlockSpec((1, 8, MOE_TILE), lambda i, *_: (i, 0, 0)),
        ),
        scratch_shapes=[
            pltpu.VMEM((SORTED_ROWS, D), BF16),
            pltpu.VMEM((chunk, D), BF16),
            pltpu.SemaphoreType.DMA((2,)),
        ],
    )
    return pl.pallas_call(
        functools.partial(_dispatch_kernel, chunk=chunk),
        out_shape=(jax.ShapeDtypeStruct((n_chunks * chunk, D), BF16),
                   jax.ShapeDtypeStruct((n_tiles, 8, MOE_TILE), F32)),
        grid_spec=grid_spec,
        compiler_params=pltpu.CompilerParams(
            dimension_semantics=("arbitrary",), vmem_limit_bytes=VMEM_LIMIT),
        name="moe_dispatch",
    )(seg_len, seg_start, tail, hf, route)


def _expert_kernel(e_ref, x_idx_ref, valid_ref, x_ref, wg_ref, wu_ref, wd_ref, y_ref):
    s = pl.program_id(0)

    @pl.when(valid_ref[s] == 1)
    def _():
        x = x_ref[...]
        g = jnp.dot(x, wg_ref[0, 0].astype(BF16), preferred_element_type=F32)
        u = jnp.dot(x, wu_ref[0, 0].astype(BF16), preferred_element_type=F32)
        y = jnp.dot((_silu(g) * u).astype(BF16), wd_ref[0, 0].astype(BF16),
                    preferred_element_type=F32)
        y_ref[...] = y.astype(BF16)

    @pl.when(valid_ref[s] == 0)
    def _():
        y_ref[...] = jnp.zeros_like(y_ref)


def _expert_call(xbuf, e_tab, x_idx_tab, valid_tab, lw, *, chunk):
    n_steps = e_tab.shape[0]
    layer = lw["layer"]
    grid_spec = pltpu.PrefetchScalarGridSpec(
        num_scalar_prefetch=3,
        grid=(n_steps,),
        in_specs=[
            pl.BlockSpec((chunk, D), lambda s, e, x, v: (x[s], 0)),
            pl.BlockSpec((1, 1, D, D_EXPERT), lambda s, e, x, v: (layer, e[s], 0, 0)),
            pl.BlockSpec((1, 1, D, D_EXPERT), lambda s, e, x, v: (layer, e[s], 0, 0)),
            pl.BlockSpec((1, 1, D_EXPERT, D), lambda s, e, x, v: (layer, e[s], 0, 0)),
        ],
        out_specs=pl.BlockSpec((chunk, D), lambda s, e, x, v: (s, 0)),
    )
    return pl.pallas_call(
        _expert_kernel,
        out_shape=jax.ShapeDtypeStruct(xbuf.shape, BF16),
        grid_spec=grid_spec,
        compiler_params=pltpu.CompilerParams(
            dimension_semantics=("arbitrary",), vmem_limit_bytes=VMEM_LIMIT),
        name="moe_experts",
    )(e_tab, x_idx_tab, valid_tab, xbuf, lw["w_gate"], lw["w_up"], lw["w_down"])


def _combine_kernel(seg_len_ref, seg_start_ref, ybuf_ref, meta_ref, x1_ref, mod_ref, gfin_ref,
                    o_ref, ys_ref, sem, *, final):
    i = pl.program_id(0)
    n_tiles = pl.num_programs(0)
    slot = i % 2

    def seg_copy(buf):
        def make(local, packed):
            return pltpu.make_async_copy(ybuf_ref.at[pl.ds(packed, SEG_ALIGN), :],
                                         ys_ref.at[buf, pl.ds(local, SEG_ALIGN), :], sem.at[buf])
        return make

    @pl.when(i == 0)
    def _():
        ys_ref[...] = jnp.zeros_like(ys_ref)
        _segment_issue(seg_len_ref, seg_start_ref, 0, seg_copy(0))

    @pl.when(i + 1 < n_tiles)
    def _():
        _segment_issue(seg_len_ref, seg_start_ref, i + 1, seg_copy(1 - slot))

    _segment_wait(seg_len_ref, i, seg_copy(slot))

    meta = meta_ref[0]
    p_lo = meta[:, 0:1].astype(I32)
    p_hi = meta[:, 1:2].astype(I32)
    w_lo = meta[:, 2:3]
    w_hi = meta[:, 3:4]
    s_idx = lax.broadcasted_iota(I32, (MOE_TILE, SORTED_ROWS), 1)
    ys = ys_ref[slot]
    y_lo = jnp.dot(jnp.where(s_idx == p_lo, 1.0, 0.0).astype(BF16), ys, preferred_element_type=F32)
    y_hi = jnp.dot(jnp.where(s_idx == p_hi, 1.0, 0.0).astype(BF16), ys, preferred_element_type=F32)
    moe = w_lo * y_lo + w_hi * y_hi

    nbt, t_tile, bt, _ = x1_ref.shape
    gt2 = mod_ref[5].reshape(nbt, 1, bt, D)
    x2 = x1_ref[...] + gt2 * moe.reshape(nbt, t_tile, bt, D)
    if final:
        ms = jnp.mean(x2 * x2, axis=-1, keepdims=True)
        x2 = x2 * lax.rsqrt(ms + RMS_EPS) * gfin_ref[...]
    o_ref[...] = x2


def _combine_call(ybuf, meta_t, x1, mod, g_final, seg_len, seg_start, *, final):
    nb, t_len, bt, _ = x1.shape
    if nb * t_len * bt == MOE_TILE:
        nbt, t_tile = nb, t_len
    else:
        nbt, t_tile = 1, MOE_TILE // bt
    n_t = t_len // t_tile
    grid_spec = pltpu.PrefetchScalarGridSpec(
        num_scalar_prefetch=2,
        grid=((nb // nbt) * n_t,),
        in_specs=[
            pl.BlockSpec(memory_space=pl.ANY),
            pl.BlockSpec((1, MOE_TILE, 8), lambda i, *_: (i, 0, 0)),
            pl.BlockSpec((nbt, t_tile, bt, D), lambda i, *_: (i // n_t, i % n_t, 0, 0)),
            pl.BlockSpec((6, nbt * bt, D), lambda i, *_: (0, i // n_t, 0)),
            pl.BlockSpec((1, D), lambda i, *_: (0, 0)),
        ],
        out_specs=pl.BlockSpec((nbt, t_tile, bt, D), lambda i, *_: (i // n_t, i % n_t, 0, 0)),
        scratch_shapes=[
            pltpu.VMEM((2, SORTED_ROWS, D), BF16),
            pltpu.SemaphoreType.DMA((2,)),
        ],
    )
    return pl.pallas_call(
        functools.partial(_combine_kernel, final=final),
        out_shape=jax.ShapeDtypeStruct((nb, t_len, bt, D), F32),
        grid_spec=grid_spec,
        compiler_params=pltpu.CompilerParams(
            dimension_semantics=("arbitrary",), vmem_limit_bytes=VMEM_LIMIT),
        name="moe_combine_final" if final else "moe_combine",
    )(seg_len, seg_start, ybuf, meta_t, x1, mod, g_final)


def _moe_sparse(hf, route, count, x1, mod, lw, g_final, *, final):
    n_tok = hf.shape[0]
    n_tiles = n_tok // MOE_TILE
    route = route.transpose(0, 1, 3, 2).reshape(n_tiles, MOE_TILE, 2 * N_EXPERTS).transpose(0, 2, 1)

    count = count[..., 0].astype(I32).reshape(n_tiles, -1, N_EXPERTS).sum(axis=1)
    chunk = EXPERT_CHUNK if n_tok >= N_EXPERTS * EXPERT_CHUNK else SMALL_EXPERT_CHUNK
    seg_len = (count + SEG_ALIGN - 1) // SEG_ALIGN * SEG_ALIGN
    total = jnp.sum(seg_len, axis=0)
    n_chunks = (total + chunk - 1) // chunk
    chunk_end = jnp.cumsum(n_chunks)
    used = chunk_end[-1]
    base = (chunk_end - n_chunks) * chunk
    seg_start = base[None, :] + jnp.cumsum(seg_len, axis=0) - seg_len
    max_chunks = -(-(TOP_K * n_tok + N_EXPERTS * SEG_ALIGN * n_tiles) // chunk) + N_EXPERTS
    step = jnp.arange(max_chunks, dtype=I32)
    valid = step < used
    x_idx = jnp.minimum(step, used - 1)
    e_tab = jnp.minimum(jnp.sum(x_idx[:, None] >= chunk_end[None, :], axis=1), N_EXPERTS - 1)
    tail = jnp.concatenate([base + total, (n_chunks * chunk - total) // SEG_ALIGN, used[None]])

    as_i32 = lambda v: v.reshape(-1).astype(I32)
    xbuf, meta = _dispatch_call(hf, route, as_i32(seg_len), as_i32(seg_start), as_i32(tail),
                                chunk=chunk, n_chunks=max_chunks)
    ybuf = _expert_call(xbuf, as_i32(e_tab), as_i32(x_idx), as_i32(valid), lw, chunk=chunk)
    meta_t = meta.transpose(0, 2, 1)
    return _combine_call(ybuf, meta_t, x1, mod, g_final, as_i32(seg_len), as_i32(seg_start),
                         final=final)


def _block_diag(w):
    heads_per_block = GATE_BLOCK // w.shape[-1]
    w = w.reshape(N_GATE_BLOCKS, heads_per_block, w.shape[-2], w.shape[-1])
    eye = jnp.eye(heads_per_block, dtype=w.dtype)
    out = jnp.einsum("ghij,hk->ghikj", w, eye)
    return out.reshape(N_GATE_BLOCKS, GATE_BLOCK, GATE_BLOCK).astype(BF16)


def _to_tiled(x, bt):
    b, t, d = x.shape
    return x.reshape(b // bt, bt, t, d).transpose(0, 2, 1, 3)


def _from_tiled(x):
    nb, t, bt, d = x.shape
    return x.transpose(0, 2, 1, 3).reshape(nb * bt, t, d)


def kernel(x_prompt, x_sample, state_lru_h, state_lru_conv, state_conf_conv, c_prompt, c_sample, w_ada, b_ada, g_mix, g_ffn, w_in, b_in, lru_conv_w, lru_conv_b, lru_wa, lru_ba, lru_wx, lru_bx, lru_lambda, w_lru_o, conf_conv_w, conf_conv_b, conf_ln_g, conf_ln_b, w_conf_o, w_out, w_router, b_router, w_gate, w_up, w_down, g_final):
    bp = x_prompt.shape[0]
    bs = x_sample.shape[0]
    btp = min(bp, MAX_BATCH_TILE)
    bts = min(bs, MAX_BATCH_TILE)
    row = lambda v: v.reshape(1, -1)
    sub8 = lambda w: jnp.broadcast_to(w[:, None, :], (w.shape[0], 8, w.shape[1]))

    c_all = jnp.concatenate([c_prompt, c_sample], axis=0)
    mod = _ada_call(c_all, w_ada.astype(BF16), b_ada.reshape(DEPTH, 1, 6 * D))
    mod = mod.reshape(DEPTH, bp + bs, 6, D).transpose(0, 2, 1, 3)

    wr_t = w_router.T
    wr_hi = wr_t.astype(BF16)
    wr_lo = (wr_t - wr_hi.astype(F32)).astype(BF16)
    g_fin = row(g_final)

    xp = _to_tiled(x_prompt, btp)
    xs = _to_tiled(x_sample, bts)

    outs_p, outs_s = [], []
    for l in range(DEPTH):
        lw = {
            "g_mix": row(g_mix[l]), "g_ffn": row(g_ffn[l]),
            "w_in": w_in[l].astype(BF16), "b_in": row(b_in[l]),
            "lru_conv_w": sub8(lru_conv_w[l]), "lru_conv_b": row(lru_conv_b[l]),
            "wa": _block_diag(lru_wa[l]), "lru_ba": row(lru_ba[l]),
            "wx": _block_diag(lru_wx[l]), "lru_bx": row(lru_bx[l]),
            "lru_lambda": row(lru_lambda[l]), "w_lru_o": w_lru_o[l].astype(BF16),
            "conf_conv_w": sub8(conf_conv_w[l]), "conf_conv_b": row(conf_conv_b[l]),
            "conf_ln_g": row(conf_ln_g[l]), "conf_ln_b": row(conf_ln_b[l]),
            "w_conf_o": w_conf_o[l].astype(BF16), "w_out": w_out[l].astype(BF16),
            "wr_hi": wr_hi, "wr_lo": wr_lo, "b_router": b_router.reshape(N_EXPERTS, 1),
            "w_gate": w_gate, "w_up": w_up, "w_down": w_down, "layer": l,
        }
        final = l == DEPTH - 1
        groups = (
            (xp, mod[l, :, :bp], jnp.zeros((bp // btp, btp, D), F32),
             jnp.zeros((bp // btp, LRU_TAPS - 1, btp, D), F32),
             jnp.zeros((bp // btp, CONF_TAPS - 1, btp, D), F32), True),
            (xs, mod[l, :, bp:], state_lru_h[l].reshape(bs // bts, bts, D),
             _to_tiled(state_lru_conv[l], bts), _to_tiled(state_conf_conv[l], bts), False),
        )
        new_x = []
        for (xg, modg, h0, lbuf, cbuf, reset), outs in zip(groups, (outs_p, outs_s)):
            x1, hf, route, count, hlast, nlbuf, ncbuf = _mix_call(
                xg, modg, h0, lbuf, cbuf, lw, reset=reset)
            new_x.append(_moe_sparse(hf.reshape(-1, D), route, count, x1, modg, lw, g_fin, final=final))
            outs.append((hlast.reshape(-1, D), _from_tiled(nlbuf), _from_tiled(ncbuf)))
        xp, xs = new_x

    stack = lambda outs, k: jnp.stack([o[k] for o in outs])
    return (_from_tiled(xp), _from_tiled(xs),
            stack(outs_p, 0), stack(outs_p, 1), stack(outs_p, 2),
            stack(outs_s, 0), stack(outs_s, 1), stack(outs_s, 2))
```

```python
import functools

import jax
import jax.numpy as jnp
from jax import lax
from jax.experimental import pallas as pl
from jax.experimental.pallas import tpu as pltpu

F32 = jnp.float32
BF16 = jnp.bfloat16
I32 = jnp.int32

D = 1024
DEPTH = 2
N_EXPERTS = 16
EXPERTS_PER_GROUP = 4
N_GROUPS = N_EXPERTS // EXPERTS_PER_GROUP
TOP_K = 2
D_EXPERT = D // 2
LRU_TAPS = 4
CONF_TAPS = 31
LRU_C = 8.0
RMS_EPS = 1e-6
LN_EPS = 1e-5
GATE_BLOCK = 256
N_GATE_BLOCKS = D // GATE_BLOCK
MIX_ROWS = 512
MAX_BATCH_TILE = 32
CONV_CHUNK_ROWS = 128
CONV_CHUNK_LANES = 128
MOE_TILE = 512
SEG_ALIGN = 16
SEG_BLOCK = 64
SORTED_ROWS = TOP_K * MOE_TILE + N_EXPERTS * SEG_ALIGN
EXPERT_CHUNK = 512
SMALL_EXPERT_CHUNK = 128
VMEM_LIMIT = 56 * 1024 * 1024


def _sigmoid(x):
    return 1.0 / (1.0 + jnp.exp(-x))


def _silu(x):
    return x * _sigmoid(x)


def _gelu_tanh(x):
    c = 0.7978845608028654
    return 0.5 * x * (1.0 + jnp.tanh(c * (x + 0.044715 * (x * x * x))))


def _const_spec(shape):
    zeros = (0,) * len(shape)
    return pl.BlockSpec(shape, lambda *_: zeros, pipeline_mode=pl.Buffered(1))


def _ada_kernel(c_ref, w_ref, b_ref, o_ref):
    s = _silu(c_ref[...]).astype(BF16)
    o_ref[0] = jnp.dot(s, w_ref[0].astype(BF16), preferred_element_type=F32) + b_ref[0]


def _ada_call(c_all, w_ada, b_ada):
    nb = c_all.shape[0]
    tn = 1536
    return pl.pallas_call(
        _ada_kernel,
        out_shape=jax.ShapeDtypeStruct((DEPTH, nb, 6 * D), F32),
        grid=(DEPTH, 6 * D // tn),
        in_specs=[
            pl.BlockSpec((nb, D), lambda l, j: (0, 0)),
            pl.BlockSpec((1, D, tn), lambda l, j: (l, 0, j)),
            pl.BlockSpec((1, 1, tn), lambda l, j: (l, 0, j)),
        ],
        out_specs=pl.BlockSpec((1, nb, tn), lambda l, j: (l, 0, j)),
        name="adaln",
    )(c_all, w_ada, b_ada)


def _depthwise_conv(full_ref, w_ref, out_ref, *, taps, t_tile, batch):
    tc = max(1, CONV_CHUNK_ROWS // batch)
    n_chunks = t_tile // tc
    groups = tc * batch // 8
    for l0 in range(0, D, CONV_CHUNK_LANES):
        lanes = slice(l0, l0 + CONV_CHUNK_LANES)
        w = [w_ref[k, :, lanes] for k in range(taps)]

        def chunk(c, carry, lanes=lanes, w=w):
            t0 = c * tc
            acc = jnp.zeros((groups, 8, CONV_CHUNK_LANES), F32)
            for k in range(taps):
                rows = full_ref[pl.ds(t0 + k, tc), :, lanes].reshape(groups, 8, CONV_CHUNK_LANES)
                acc = acc + rows * w[k]
            out_ref[pl.ds(t0, tc), :, lanes] = acc.reshape(tc, batch, CONV_CHUNK_LANES)
            return carry

        lax.fori_loop(0, n_chunks, chunk, 0)


def _route(logits):
    m = logits[0]
    for e in range(1, N_EXPERTS):
        m = jnp.maximum(m, logits[e])
    ex = [jnp.exp(l - m) for l in logits]
    den = ex[0]
    for e in range(1, N_EXPERTS):
        den = den + ex[e]
    p = [v / den for v in ex]
    sums = []
    for g in range(N_GROUPS):
        q = p[g * EXPERTS_PER_GROUP:(g + 1) * EXPERTS_PER_GROUP]
        m1 = jnp.maximum(jnp.maximum(q[0], q[1]), jnp.maximum(q[2], q[3]))
        m2 = None
        for i in range(EXPERTS_PER_GROUP):
            for j in range(i + 1, EXPERTS_PER_GROUP):
                mn = jnp.minimum(q[i], q[j])
                m2 = mn if m2 is None else jnp.maximum(m2, mn)
        sums.append(m1 + m2)
    best = sums[0]
    grp = jnp.zeros(best.shape, I32)
    for g in range(1, N_GROUPS):
        take = sums[g] > best
        best = jnp.where(take, sums[g], best)
        grp = jnp.where(take, g, grp)
    combine, selected = [], []
    for g in range(N_GROUPS):
        q = p[g * EXPERTS_PER_GROUP:(g + 1) * EXPERTS_PER_GROUP]
        in_grp = grp == g
        for i in range(EXPERTS_PER_GROUP):
            rank = jnp.zeros(best.shape, I32)
            for j in range(EXPERTS_PER_GROUP):
                if j == i:
                    continue
                ahead = (q[j] > q[i]) if j > i else (q[j] >= q[i])
                rank = rank + ahead.astype(I32)
            sel = jnp.where(in_grp, rank, TOP_K) < TOP_K
            combine.append(jnp.where(sel, q[i] / best, 0.0))
            selected.append(jnp.where(sel, 1.0, 0.0))
    return combine, selected


def _mix_kernel(x_ref, mod_ref, h0_ref, lbuf_ref, cbuf_ref,
                gmix_ref, gffn_ref, win_ref, bin_ref,
                lcw_ref, lcb_ref, wa_ref, ba_ref, wx_ref, bx_ref, lam_ref, wlo_ref,
                ccw_ref, ccb_ref, lng_ref, lnb_ref, wco_ref, wout_ref,
                wrh_ref, wrl_ref, br_ref,
                x1_ref, hf_ref, route_ref, count_ref, hlast_ref, nlbuf_ref, ncbuf_ref,
                fu_ref, fg_ref, a_ref, b_ref, hcar_ref,
                *, t_tile, batch, reset):
    j = pl.program_id(1)
    n_t = pl.num_programs(1)
    rows = t_tile * batch

    @pl.when(j == 0)
    def _():
        fu_ref[0:LRU_TAPS - 1] = lbuf_ref[0]
        fg_ref[0:CONF_TAPS - 1] = cbuf_ref[0]
        hcar_ref[...] = h0_ref[0]

    x = x_ref[0]
    sh1, sc1, gt1 = mod_ref[0], mod_ref[1], mod_ref[2]
    sh2, sc2 = mod_ref[3], mod_ref[4]

    ms = jnp.mean(x * x, axis=-1, keepdims=True)
    hm = (x * lax.rsqrt(ms + RMS_EPS) * gmix_ref[...]) * (1.0 + sc1) + sh1
    hm2 = hm.reshape(rows, D).astype(BF16)

    def proj(k):
        cols = slice(k * D, (k + 1) * D)
        return jnp.dot(hm2, win_ref[:, cols], preferred_element_type=F32) + bin_ref[:, cols]

    fu_ref[LRU_TAPS - 1:] = proj(0).reshape(t_tile, batch, D)
    _depthwise_conv(fu_ref, lcw_ref, a_ref, taps=LRU_TAPS, t_tile=t_tile, batch=batch)
    xc = a_ref[...].reshape(rows, D) + lcb_ref[...]
    xcb = xc.astype(BF16)
    r_parts, i_parts = [], []
    for blk in range(N_GATE_BLOCKS):
        cols = slice(blk * GATE_BLOCK, (blk + 1) * GATE_BLOCK)
        r_parts.append(jnp.dot(xcb[:, cols], wa_ref[blk], preferred_element_type=F32))
        i_parts.append(jnp.dot(xcb[:, cols], wx_ref[blk], preferred_element_type=F32))
    r_gate = _sigmoid(jnp.concatenate(r_parts, axis=-1) + ba_ref[...])
    i_gate = _sigmoid(jnp.concatenate(i_parts, axis=-1) + bx_ref[...])
    neg_lam = -lam_ref[...]
    softplus = jnp.maximum(neg_lam, 0.0) + jnp.log1p(jnp.exp(-jnp.abs(neg_lam)))
    log_a = (-LRU_C) * r_gate * softplus
    a = jnp.exp(log_a)
    mult = jnp.sqrt(-jnp.tanh(log_a) * (a * a + 1.0))
    a = a.reshape(t_tile, batch, D)
    mult = mult.reshape(t_tile, batch, D)
    if reset:
        t_idx = lax.broadcasted_iota(I32, (t_tile, batch, D), 0)
        first = jnp.logical_and(t_idx == 0, j == 0)
        a = jnp.where(first, 0.0, a)
        mult = jnp.where(first, 1.0, mult)
    a_ref[...] = a
    b_ref[...] = mult * (i_gate * xc).reshape(t_tile, batch, D)

    def scan_step(t, carry):
        h = a_ref[t] * hcar_ref[...] + b_ref[t]
        hcar_ref[...] = h
        b_ref[t] = h
        return carry

    lax.fori_loop(0, t_tile, scan_step, 0)
    hseq = b_ref[...].reshape(rows, D)
    out_a = jnp.dot((hseq * _gelu_tanh(proj(1))).astype(BF16), wlo_ref[...],
                    preferred_element_type=F32)

    glu = proj(2) * _sigmoid(proj(3))
    fg_ref[CONF_TAPS - 1:] = glu.reshape(t_tile, batch, D)
    _depthwise_conv(fg_ref, ccw_ref, a_ref, taps=CONF_TAPS, t_tile=t_tile, batch=batch)
    dc = a_ref[...].reshape(rows, D) + ccb_ref[...]
    mu = jnp.mean(dc, axis=-1, keepdims=True)
    dcc = dc - mu
    var = jnp.mean(dcc * dcc, axis=-1, keepdims=True)
    ln = dcc * lax.rsqrt(var + LN_EPS) * lng_ref[...] + lnb_ref[...]
    out_b = jnp.dot(_silu(ln).astype(BF16), wco_ref[...], preferred_element_type=F32)

    merged = _sigmoid(proj(4)) * out_a + _sigmoid(proj(5)) * out_b
    mo = jnp.dot(merged.astype(BF16), wout_ref[...], preferred_element_type=F32)
    x1 = x + gt1 * mo.reshape(t_tile, batch, D)
    x1_ref[0] = x1

    ms2 = jnp.mean(x1 * x1, axis=-1, keepdims=True)
    hf = ((x1 * lax.rsqrt(ms2 + RMS_EPS) * gffn_ref[...]) * (1.0 + sc2) + sh2).reshape(rows, D)
    hf_hi = hf.astype(BF16)
    hf_ref[0] = hf_hi
    hf_lo = (hf - hf_hi.astype(F32)).astype(BF16)
    nt = (((1,), (1,)), ((), ()))
    logits = (lax.dot_general(wrh_ref[...], hf_hi, nt, preferred_element_type=F32)
              + lax.dot_general(wrh_ref[...], hf_lo, nt, preferred_element_type=F32)
              + lax.dot_general(wrl_ref[...], hf_hi, nt, preferred_element_type=F32)
              + br_ref[...])
    combine, selected = _route([logits[e:e + 1, :] for e in range(N_EXPERTS)])
    route_ref[0, 0] = jnp.concatenate(combine + selected, axis=0)
    n_sel = jnp.sum(jnp.concatenate(selected, axis=0), axis=-1, keepdims=True)
    count_ref[0, 0] = jnp.broadcast_to(n_sel, (N_EXPERTS, 128))

    fu_ref[0:LRU_TAPS - 1] = fu_ref[t_tile:t_tile + LRU_TAPS - 1]
    fg_ref[0:CONF_TAPS - 1] = fg_ref[t_tile:t_tile + CONF_TAPS - 1]

    @pl.when(j == n_t - 1)
    def _():
        hlast_ref[0] = hcar_ref[...]
        nlbuf_ref[0] = fu_ref[0:LRU_TAPS - 1]
        ncbuf_ref[0] = fg_ref[0:CONF_TAPS - 1]


def _mix_call(x, mod, h0, lbuf, cbuf, lw, *, reset):
    nb, t_len, bt, _ = x.shape
    t_tile = min(t_len, MIX_ROWS // bt)
    n_t = t_len // t_tile
    rows = t_tile * bt
    kern = functools.partial(_mix_kernel, t_tile=t_tile, batch=bt, reset=reset)
    consts = [lw["g_mix"], lw["g_ffn"], lw["w_in"], lw["b_in"],
              lw["lru_conv_w"], lw["lru_conv_b"], lw["wa"], lw["lru_ba"], lw["wx"], lw["lru_bx"],
              lw["lru_lambda"], lw["w_lru_o"], lw["conf_conv_w"], lw["conf_conv_b"],
              lw["conf_ln_g"], lw["conf_ln_b"], lw["w_conf_o"], lw["w_out"],
              lw["wr_hi"], lw["wr_lo"], lw["b_router"]]
    in_specs = [
        pl.BlockSpec((1, t_tile, bt, D), lambda b, j: (b, j, 0, 0)),
        pl.BlockSpec((6, bt, D), lambda b, j: (0, b, 0)),
        pl.BlockSpec((1, bt, D), lambda b, j: (b, 0, 0)),
        pl.BlockSpec((1, LRU_TAPS - 1, bt, D), lambda b, j: (b, 0, 0, 0)),
        pl.BlockSpec((1, CONF_TAPS - 1, bt, D), lambda b, j: (b, 0, 0, 0)),
    ]
    in_specs += [_const_spec(c.shape) for c in consts]
    out_shape = (
        jax.ShapeDtypeStruct((nb, t_len, bt, D), F32),
        jax.ShapeDtypeStruct((nb * n_t, rows, D), BF16),
        jax.ShapeDtypeStruct((nb, n_t, 2 * N_EXPERTS, rows), F32),
        jax.ShapeDtypeStruct((nb, n_t, N_EXPERTS, 128), F32),
        jax.ShapeDtypeStruct((nb, bt, D), F32),
        jax.ShapeDtypeStruct((nb, LRU_TAPS - 1, bt, D), F32),
        jax.ShapeDtypeStruct((nb, CONF_TAPS - 1, bt, D), F32),
    )
    out_specs = (
        pl.BlockSpec((1, t_tile, bt, D), lambda b, j: (b, j, 0, 0)),
        pl.BlockSpec((1, rows, D), lambda b, j: (b * n_t + j, 0, 0)),
        pl.BlockSpec((1, 1, 2 * N_EXPERTS, rows), lambda b, j: (b, j, 0, 0)),
        pl.BlockSpec((1, 1, N_EXPERTS, 128), lambda b, j: (b, j, 0, 0)),
        pl.BlockSpec((1, bt, D), lambda b, j: (b, 0, 0)),
        pl.BlockSpec((1, LRU_TAPS - 1, bt, D), lambda b, j: (b, 0, 0, 0)),
        pl.BlockSpec((1, CONF_TAPS - 1, bt, D), lambda b, j: (b, 0, 0, 0)),
    )
    scratch = [
        pltpu.VMEM((t_tile + LRU_TAPS - 1, bt, D), F32),
        pltpu.VMEM((t_tile + CONF_TAPS - 1, bt, D), F32),
        pltpu.VMEM((t_tile, bt, D), F32),
        pltpu.VMEM((t_tile, bt, D), F32),
        pltpu.VMEM((bt, D), F32),
    ]
    return pl.pallas_call(
        kern,
        out_shape=out_shape,
        grid=(nb, n_t),
        in_specs=in_specs,
        out_specs=out_specs,
        scratch_shapes=scratch,
        compiler_params=pltpu.CompilerParams(
            dimension_semantics=("arbitrary", "arbitrary"), vmem_limit_bytes=VMEM_LIMIT),
        name="mix_reset" if reset else "mix_cont",
    )(x, mod, h0, lbuf, cbuf, *consts)


def _segment_issue(seg_len_ref, seg_start_ref, tile, make_copy):
    off = 0
    for e in range(N_EXPERTS):
        length = seg_len_ref[tile * N_EXPERTS + e]
        start = seg_start_ref[tile * N_EXPERTS + e]
        n_big = length // SEG_BLOCK
        done = n_big * SEG_BLOCK

        def issue_big(k, carry, off=off, start=start):
            local = pl.multiple_of(off + k * SEG_BLOCK, SEG_ALIGN)
            packed = pl.multiple_of(start + k * SEG_BLOCK, SEG_ALIGN)
            make_copy(local, packed, SEG_BLOCK).start()
            return carry

        def issue_small(k, carry, off=off + done, start=start + done):
            local = pl.multiple_of(off + k * SEG_ALIGN, SEG_ALIGN)
            packed = pl.multiple_of(start + k * SEG_ALIGN, SEG_ALIGN)
            make_copy(local, packed, SEG_ALIGN).start()
            return carry

        lax.fori_loop(0, n_big, issue_big, 0)
        lax.fori_loop(0, (length - done) // SEG_ALIGN, issue_small, 0)
        off = off + length


def _segment_wait(seg_len_ref, tile, make_copy):
    n_big, n_small = 0, 0
    for e in range(N_EXPERTS):
        length = seg_len_ref[tile * N_EXPERTS + e]
        n_big = n_big + length // SEG_BLOCK
        n_small = n_small + (length % SEG_BLOCK) // SEG_ALIGN

    def wait_big(k, carry):
        make_copy(0, 0, SEG_BLOCK).wait()
        return carry

    def wait_small(k, carry):
        make_copy(0, 0, SEG_ALIGN).wait()
        return carry

    lax.fori_loop(0, n_big, wait_big, 0)
    lax.fori_loop(0, n_small, wait_small, 0)


def _dispatch_kernel(seg_len_ref, seg_start_ref, tail_ref, *refs, chunk, group_tiles):
    hf_refs = refs[:len(group_tiles)]
    route_ref, xbuf_ref, meta_ref, xs_ref, zero_ref, sem = refs[len(group_tiles):]
    i = pl.program_id(0)
    last = pl.num_programs(0) - 1
    comb = route_ref[0, 0:N_EXPERTS, :]
    sel = route_ref[0, N_EXPERTS:2 * N_EXPERTS, :]
    on = sel > 0.0

    r_idx = lax.broadcasted_iota(I32, (MOE_TILE, MOE_TILE), 0)
    c_idx = lax.broadcasted_iota(I32, (MOE_TILE, MOE_TILE), 1)
    upper = jnp.where(r_idx <= c_idx, 1.0, 0.0).astype(BF16)
    csum = jnp.dot(sel.astype(BF16), upper, preferred_element_type=F32)
    rank = csum - sel
    count = csum[:, MOE_TILE - 1:MOE_TILE].astype(I32)
    padded = ((count + (SEG_ALIGN - 1)) // SEG_ALIGN) * SEG_ALIGN
    e_r = lax.broadcasted_iota(I32, (N_EXPERTS, N_EXPERTS), 0)
    e_c = lax.broadcasted_iota(I32, (N_EXPERTS, N_EXPERTS), 1)
    lower = jnp.where(e_c < e_r, 1.0, 0.0).astype(BF16)
    padded_b = jnp.broadcast_to(padded.astype(F32), (N_EXPERTS, 128)).astype(BF16)
    seg_off = jnp.dot(lower, padded_b, preferred_element_type=F32)[:, 0:1]

    pos = seg_off + rank
    p_lo = jnp.min(jnp.where(on, pos, 4.0 * SORTED_ROWS), axis=0, keepdims=True)
    p_hi = jnp.max(jnp.where(on, pos, -1.0), axis=0, keepdims=True)
    w_lo = jnp.sum(jnp.where(jnp.logical_and(on, pos == p_lo), comb, 0.0), axis=0, keepdims=True)
    w_hi = jnp.sum(jnp.where(jnp.logical_and(on, pos == p_hi), comb, 0.0), axis=0, keepdims=True)
    meta_ref[0] = jnp.concatenate(
        [p_lo, p_hi, w_lo, w_hi, jnp.zeros((4, MOE_TILE), F32)], axis=0)

    s_idx = lax.broadcasted_iota(I32, (SORTED_ROWS, MOE_TILE), 0)
    one_hot = jnp.where(s_idx == p_lo.astype(I32), 1.0,
                        jnp.where(s_idx == p_hi.astype(I32), 1.0, 0.0)).astype(BF16)
    hf = hf_refs[0][...]
    first_tile = group_tiles[0]
    for g in range(1, len(group_tiles)):
        hf = jnp.where(i >= first_tile, hf_refs[g][...], hf)
        first_tile += group_tiles[g]
    sorted_rows = jnp.dot(one_hot, hf, preferred_element_type=F32).astype(BF16)

    def seg_copy(local, packed, n_rows):
        return pltpu.make_async_copy(xs_ref.at[pl.ds(local, n_rows), :],
                                     xbuf_ref.at[pl.ds(packed, n_rows), :], sem.at[0])

    @pl.when(i > 0)
    def _():
        _segment_wait(seg_len_ref, i - 1, seg_copy)

    xs_ref[...] = sorted_rows
    _segment_issue(seg_len_ref, seg_start_ref, i, seg_copy)

    @pl.when(i == last)
    def _():
        _segment_wait(seg_len_ref, i, seg_copy)
        zero_ref[...] = jnp.zeros_like(zero_ref)

        def pad_copy(row):
            return pltpu.make_async_copy(zero_ref.at[pl.ds(0, SEG_ALIGN), :],
                                         xbuf_ref.at[pl.ds(row, SEG_ALIGN), :], sem.at[1])

        n_pad = 0
        for e in range(N_EXPERTS):
            def issue(k, carry, e=e):
                pad_copy(pl.multiple_of(tail_ref[e] + k * SEG_ALIGN, SEG_ALIGN)).start()
                return carry

            lax.fori_loop(0, tail_ref[N_EXPERTS + e], issue, 0)
            n_pad = n_pad + tail_ref[N_EXPERTS + e]

        def wait_pad(k, carry):
            pad_copy(0).wait()
            return carry

        lax.fori_loop(0, n_pad, wait_pad, 0)

        def idle_copy(c):
            return pltpu.make_async_copy(
                zero_ref, xbuf_ref.at[pl.ds(pl.multiple_of(c * chunk, chunk), chunk), :], sem.at[1])

        def issue_idle(c, carry):
            idle_copy(c).start()
            return carry

        def wait_idle(c, carry):
            idle_copy(c).wait()
            return carry

        n_chunks = xbuf_ref.shape[0] // chunk
        lax.fori_loop(tail_ref[2 * N_EXPERTS], n_chunks, issue_idle, 0)
        lax.fori_loop(tail_ref[2 * N_EXPERTS], n_chunks, wait_idle, 0)


def _dispatch_call(hf_groups, route, seg_len, seg_start, tail, *, chunk, n_chunks):
    n_tiles = route.shape[0]
    group_tiles = tuple(hf.shape[0] // MOE_TILE for hf in hf_groups)
    first_tiles = [sum(group_tiles[:g]) for g in range(len(group_tiles))]

    def hf_spec(first, count):
        return pl.BlockSpec((MOE_TILE, D), lambda i, *_: (jnp.clip(i - first, 0, count - 1), 0))

    grid_spec = pltpu.PrefetchScalarGridSpec(
        num_scalar_prefetch=3,
        grid=(n_tiles,),
        in_specs=[hf_spec(f, c) for f, c in zip(first_tiles, group_tiles)] + [
            pl.BlockSpec((1, 2 * N_EXPERTS, MOE_TILE), lambda i, *_: (i, 0, 0)),
        ],
        out_specs=(
            pl.BlockSpec(memory_space=pl.ANY),
            pl.BlockSpec((1, 8, MOE_TILE), lambda i, *_: (i, 0, 0)),
        ),
        scratch_shapes=[
            pltpu.VMEM((SORTED_ROWS, D), BF16),
            pltpu.VMEM((chunk, D), BF16),
            pltpu.SemaphoreType.DMA((2,)),
        ],
    )
    return pl.pallas_call(
        functools.partial(_dispatch_kernel, chunk=chunk, group_tiles=group_tiles),
        out_shape=(jax.ShapeDtypeStruct((n_chunks * chunk, D), BF16),
                   jax.ShapeDtypeStruct((n_tiles, 8, MOE_TILE), F32)),
        grid_spec=grid_spec,
        compiler_params=pltpu.CompilerParams(
            dimension_semantics=("arbitrary",), vmem_limit_bytes=VMEM_LIMIT),
        name="moe_dispatch",
    )(seg_len, seg_start, tail, *hf_groups, route)


def _expert_kernel(e_ref, x_idx_ref, valid_ref, x_ref, wg_ref, wu_ref, wd_ref, y_ref,
                   wg_bf, wu_bf, wd_bf):
    s = pl.program_id(0)

    @pl.when(jnp.logical_or(s == 0, e_ref[s] != e_ref[jnp.maximum(s - 1, 0)]))
    def _():
        wg_bf[...] = wg_ref[0, 0].astype(BF16)
        wu_bf[...] = wu_ref[0, 0].astype(BF16)
        wd_bf[...] = wd_ref[0, 0].astype(BF16)

    @pl.when(valid_ref[s] == 1)
    def _():
        x = x_ref[...]
        g = jnp.dot(x, wg_bf[...], preferred_element_type=F32)
        u = jnp.dot(x, wu_bf[...], preferred_element_type=F32)
        y = jnp.dot((_silu(g) * u).astype(BF16), wd_bf[...], preferred_element_type=F32)
        y_ref[...] = y.astype(BF16)

    @pl.when(valid_ref[s] == 0)
    def _():
        y_ref[...] = jnp.zeros_like(y_ref)


def _expert_call(xbuf, e_tab, x_idx_tab, valid_tab, lw, *, chunk):
    n_steps = e_tab.shape[0]
    layer = lw["layer"]
    grid_spec = pltpu.PrefetchScalarGridSpec(
        num_scalar_prefetch=3,
        grid=(n_steps,),
        in_specs=[
            pl.BlockSpec((chunk, D), lambda s, e, x, v: (x[s], 0)),
            pl.BlockSpec((1, 1, D, D_EXPERT), lambda s, e, x, v: (layer, e[s], 0, 0)),
            pl.BlockSpec((1, 1, D, D_EXPERT), lambda s, e, x, v: (layer, e[s], 0, 0)),
            pl.BlockSpec((1, 1, D_EXPERT, D), lambda s, e, x, v: (layer, e[s], 0, 0)),
        ],
        out_specs=pl.BlockSpec((chunk, D), lambda s, e, x, v: (s, 0)),
        scratch_shapes=[
            pltpu.VMEM((D, D_EXPERT), BF16),
            pltpu.VMEM((D, D_EXPERT), BF16),
            pltpu.VMEM((D_EXPERT, D), BF16),
        ],
    )
    return pl.pallas_call(
        _expert_kernel,
        out_shape=jax.ShapeDtypeStruct(xbuf.shape, BF16),
        grid_spec=grid_spec,
        compiler_params=pltpu.CompilerParams(
            dimension_semantics=("arbitrary",), vmem_limit_bytes=VMEM_LIMIT),
        name="moe_experts",
    )(e_tab, x_idx_tab, valid_tab, xbuf, lw["w_gate"], lw["w_up"], lw["w_down"])


def _combine_kernel(seg_len_ref, seg_start_ref, ybuf_ref, meta_ref, x1_ref, mod_ref, gfin_ref,
                    o_ref, ys_ref, sem, *, final, first_tile):
    i = pl.program_id(0)
    n_tiles = pl.num_programs(0)
    slot = i % 2
    tile = first_tile + i

    def seg_copy(buf):
        def make(local, packed, n_rows):
            return pltpu.make_async_copy(ybuf_ref.at[pl.ds(packed, n_rows), :],
                                         ys_ref.at[buf, pl.ds(local, n_rows), :], sem.at[buf])
        return make

    @pl.when(i == 0)
    def _():
        ys_ref[...] = jnp.zeros_like(ys_ref)
        _segment_issue(seg_len_ref, seg_start_ref, tile, seg_copy(0))

    @pl.when(i + 1 < n_tiles)
    def _():
        _segment_issue(seg_len_ref, seg_start_ref, tile + 1, seg_copy(1 - slot))

    _segment_wait(seg_len_ref, tile, seg_copy(slot))

    meta = meta_ref[0]
    p_lo = meta[:, 0:1].astype(I32)
    p_hi = meta[:, 1:2].astype(I32)
    w_lo = meta[:, 2:3]
    w_hi = meta[:, 3:4]
    s_idx = lax.broadcasted_iota(I32, (MOE_TILE, SORTED_ROWS), 1)
    ys = ys_ref[slot]
    y_lo = jnp.dot(jnp.where(s_idx == p_lo, 1.0, 0.0).astype(BF16), ys, preferred_element_type=F32)
    y_hi = jnp.dot(jnp.where(s_idx == p_hi, 1.0, 0.0).astype(BF16), ys, preferred_element_type=F32)
    moe = w_lo * y_lo + w_hi * y_hi

    nbt, t_tile, bt, _ = x1_ref.shape
    gt2 = mod_ref[5].reshape(nbt, 1, bt, D)
    x2 = x1_ref[...] + gt2 * moe.reshape(nbt, t_tile, bt, D)
    if final:
        ms = jnp.mean(x2 * x2, axis=-1, keepdims=True)
        x2 = x2 * lax.rsqrt(ms + RMS_EPS) * gfin_ref[...]
    o_ref[...] = x2


def _combine_call(ybuf, meta_t, x1, mod, g_final, seg_len, seg_start, *, final, first_tile):
    nb, t_len, bt, _ = x1.shape
    if nb * t_len * bt == MOE_TILE:
        nbt, t_tile = nb, t_len
    else:
        nbt, t_tile = 1, MOE_TILE // bt
    n_t = t_len // t_tile
    grid_spec = pltpu.PrefetchScalarGridSpec(
        num_scalar_prefetch=2,
        grid=((nb // nbt) * n_t,),
        in_specs=[
            pl.BlockSpec(memory_space=pl.ANY),
            pl.BlockSpec((1, MOE_TILE, 8), lambda i, *_: (i, 0, 0)),
            pl.BlockSpec((nbt, t_tile, bt, D), lambda i, *_: (i // n_t, i % n_t, 0, 0)),
            pl.BlockSpec((6, nbt * bt, D), lambda i, *_: (0, i // n_t, 0)),
            pl.BlockSpec((1, D), lambda i, *_: (0, 0)),
        ],
        out_specs=pl.BlockSpec((nbt, t_tile, bt, D), lambda i, *_: (i // n_t, i % n_t, 0, 0)),
        scratch_shapes=[
            pltpu.VMEM((2, SORTED_ROWS, D), BF16),
            pltpu.SemaphoreType.DMA((2,)),
        ],
    )
    return pl.pallas_call(
        functools.partial(_combine_kernel, final=final, first_tile=first_tile),
        out_shape=jax.ShapeDtypeStruct((nb, t_len, bt, D), F32),
        grid_spec=grid_spec,
        compiler_params=pltpu.CompilerParams(
            dimension_semantics=("arbitrary",), vmem_limit_bytes=VMEM_LIMIT),
        name="moe_combine_final" if final else "moe_combine",
    )(seg_len, seg_start, ybuf, meta_t, x1, mod, g_final)


def _moe_sparse(groups, lw, g_final, *, final):
    hf_groups = [g[0] for g in groups]
    group_tiles = [hf.shape[0] // MOE_TILE for hf in hf_groups]
    n_tok = sum(hf.shape[0] for hf in hf_groups)
    n_tiles = sum(group_tiles)
    route = jnp.concatenate([
        g[1].transpose(0, 1, 3, 2).reshape(-1, MOE_TILE, 2 * N_EXPERTS).transpose(0, 2, 1)
        for g in groups])

    count = jnp.concatenate([
        g[2][..., 0].astype(I32).reshape(t, -1, N_EXPERTS).sum(axis=1)
        for g, t in zip(groups, group_tiles)])
    chunk = EXPERT_CHUNK if n_tok >= N_EXPERTS * EXPERT_CHUNK else SMALL_EXPERT_CHUNK
    seg_len = (count + SEG_ALIGN - 1) // SEG_ALIGN * SEG_ALIGN
    total = jnp.sum(seg_len, axis=0)
    n_chunks = (total + chunk - 1) // chunk
    chunk_end = jnp.cumsum(n_chunks)
    used = chunk_end[-1]
    base = (chunk_end - n_chunks) * chunk
    seg_start = base[None, :] + jnp.cumsum(seg_len, axis=0) - seg_len
    max_chunks = -(-(TOP_K * n_tok + N_EXPERTS * SEG_ALIGN * n_tiles) // chunk) + N_EXPERTS
    step = jnp.arange(max_chunks, dtype=I32)
    valid = step < used
    x_idx = jnp.minimum(step, used - 1)
    e_tab = jnp.minimum(jnp.sum(x_idx[:, None] >= chunk_end[None, :], axis=1), N_EXPERTS - 1)
    tail = jnp.concatenate([base + total, (n_chunks * chunk - total) // SEG_ALIGN, used[None]])

    as_i32 = lambda v: v.reshape(-1).astype(I32)
    xbuf, meta = _dispatch_call(hf_groups, route, as_i32(seg_len), as_i32(seg_start), as_i32(tail),
                                chunk=chunk, n_chunks=max_chunks)
    ybuf = _expert_call(xbuf, as_i32(e_tab), as_i32(x_idx), as_i32(valid), lw, chunk=chunk)
    meta_t = meta.transpose(0, 2, 1)
    outs, first_tile = [], 0
    for (_, _, _, x1, mod), tiles in zip(groups, group_tiles):
        outs.append(_combine_call(ybuf, meta_t[first_tile:first_tile + tiles], x1, mod, g_final,
                                  as_i32(seg_len), as_i32(seg_start), final=final,
                                  first_tile=first_tile))
        first_tile += tiles
    return outs


def _block_diag(w):
    heads_per_block = GATE_BLOCK // w.shape[-1]
    w = w.reshape(N_GATE_BLOCKS, heads_per_block, w.shape[-2], w.shape[-1])
    eye = jnp.eye(heads_per_block, dtype=w.dtype)
    out = jnp.einsum("ghij,hk->ghikj", w, eye)
    return out.reshape(N_GATE_BLOCKS, GATE_BLOCK, GATE_BLOCK).astype(BF16)


def _to_tiled(x, bt):
    b, t, d = x.shape
    return x.reshape(b // bt, bt, t, d).transpose(0, 2, 1, 3)


def _from_tiled(x):
    nb, t, bt, d = x.shape
    return x.transpose(0, 2, 1, 3).reshape(nb * bt, t, d)


def kernel(x_prompt, x_sample, state_lru_h, state_lru_conv, state_conf_conv, c_prompt, c_sample, w_ada, b_ada, g_mix, g_ffn, w_in, b_in, lru_conv_w, lru_conv_b, lru_wa, lru_ba, lru_wx, lru_bx, lru_lambda, w_lru_o, conf_conv_w, conf_conv_b, conf_ln_g, conf_ln_b, w_conf_o, w_out, w_router, b_router, w_gate, w_up, w_down, g_final):
    bp = x_prompt.shape[0]
    bs = x_sample.shape[0]
    btp = min(bp, MAX_BATCH_TILE)
    bts = min(bs, MAX_BATCH_TILE)
    row = lambda v: v.reshape(1, -1)
    sub8 = lambda w: jnp.broadcast_to(w[:, None, :], (w.shape[0], 8, w.shape[1]))

    c_all = jnp.concatenate([c_prompt, c_sample], axis=0)
    mod = _ada_call(c_all, w_ada, b_ada.reshape(DEPTH, 1, 6 * D))
    mod = mod.reshape(DEPTH, bp + bs, 6, D).transpose(0, 2, 1, 3)

    wr_t = w_router.T
    wr_hi = wr_t.astype(BF16)
    wr_lo = (wr_t - wr_hi.astype(F32)).astype(BF16)
    g_fin = row(g_final)

    xp = _to_tiled(x_prompt, btp)
    xs = _to_tiled(x_sample, bts)

    outs_p, outs_s = [], []
    for l in range(DEPTH):
        lw = {
            "g_mix": row(g_mix[l]), "g_ffn": row(g_ffn[l]),
            "w_in": w_in[l].astype(BF16), "b_in": row(b_in[l]),
            "lru_conv_w": sub8(lru_conv_w[l]), "lru_conv_b": row(lru_conv_b[l]),
            "wa": _block_diag(lru_wa[l]), "lru_ba": row(lru_ba[l]),
            "wx": _block_diag(lru_wx[l]), "lru_bx": row(lru_bx[l]),
            "lru_lambda": row(lru_lambda[l]), "w_lru_o": w_lru_o[l].astype(BF16),
            "conf_conv_w": sub8(conf_conv_w[l]), "conf_conv_b": row(conf_conv_b[l]),
            "conf_ln_g": row(conf_ln_g[l]), "conf_ln_b": row(conf_ln_b[l]),
            "w_conf_o": w_conf_o[l].astype(BF16), "w_out": w_out[l].astype(BF16),
            "wr_hi": wr_hi, "wr_lo": wr_lo, "b_router": b_router.reshape(N_EXPERTS, 1),
            "w_gate": w_gate, "w_up": w_up, "w_down": w_down, "layer": l,
        }
        final = l == DEPTH - 1
        groups = (
            (xp, mod[l, :, :bp], jnp.zeros((bp // btp, btp, D), F32),
             jnp.zeros((bp // btp, LRU_TAPS - 1, btp, D), F32),
             jnp.zeros((bp // btp, CONF_TAPS - 1, btp, D), F32), True),
            (xs, mod[l, :, bp:], state_lru_h[l].reshape(bs // bts, bts, D),
             _to_tiled(state_lru_conv[l], bts), _to_tiled(state_conf_conv[l], bts), False),
        )
        moe_groups = []
        for (xg, modg, h0, lbuf, cbuf, reset), outs in zip(groups, (outs_p, outs_s)):
            x1, hf, route, count, hlast, nlbuf, ncbuf = _mix_call(
                xg, modg, h0, lbuf, cbuf, lw, reset=reset)
            moe_groups.append((hf.reshape(-1, D), route, count, x1, modg))
            outs.append((hlast.reshape(-1, D), _from_tiled(nlbuf), _from_tiled(ncbuf)))
        xp, xs = _moe_sparse(moe_groups, lw, g_fin, final=final)

    stack = lambda outs, k: jnp.stack([o[k] for o in outs])
    return (_from_tiled(xp), _from_tiled(xs),
            stack(outs_p, 0), stack(outs_p, 1), stack(outs_p, 2),
            stack(outs_s, 0), stack(outs_s, 1), stack(outs_s, 2))
```

```python
import functools

import jax
import jax.numpy as jnp
from jax import lax
from jax.experimental import pallas as pl
from jax.experimental.pallas import tpu as pltpu

F32 = jnp.float32
BF16 = jnp.bfloat16
I32 = jnp.int32

D = 1024
DEPTH = 2
N_EXPERTS = 16
EXPERTS_PER_GROUP = 4
N_GROUPS = N_EXPERTS // EXPERTS_PER_GROUP
TOP_K = 2
D_EXPERT = D // 2
LRU_TAPS = 4
CONF_TAPS = 31
LRU_C = 8.0
RMS_EPS = 1e-6
LN_EPS = 1e-5
GATE_BLOCK = 256
N_GATE_BLOCKS = D // GATE_BLOCK
MIX_ROWS = 512
MAX_BATCH_TILE = 32
CONV_CHUNK_ROWS = 128
CONV_CHUNK_LANES = 128
MOE_TILE = 512
SEG_ALIGN = 16
SEG_BLOCK = 64
SORTED_ROWS = TOP_K * MOE_TILE + N_EXPERTS * SEG_ALIGN
EXPERT_CHUNK = 512
SMALL_EXPERT_CHUNK = 128
VMEM_LIMIT = 56 * 1024 * 1024


def _sigmoid(x):
    return 1.0 / (1.0 + jnp.exp(-x))


def _silu(x):
    return x * _sigmoid(x)


def _gelu_tanh(x):
    c = 0.7978845608028654
    return 0.5 * x * (1.0 + jnp.tanh(c * (x + 0.044715 * (x * x * x))))


def _const_spec(shape):
    zeros = (0,) * len(shape)
    return pl.BlockSpec(shape, lambda *_: zeros, pipeline_mode=pl.Buffered(1))


def _ada_kernel(c_ref, w_ref, b_ref, o_ref):
    s = _silu(c_ref[...]).astype(BF16)
    o_ref[0] = jnp.dot(s, w_ref[0].astype(BF16), preferred_element_type=F32) + b_ref[0]


def _ada_call(c_all, w_ada, b_ada):
    nb = c_all.shape[0]
    tn = 1536
    return pl.pallas_call(
        _ada_kernel,
        out_shape=jax.ShapeDtypeStruct((DEPTH, nb, 6 * D), F32),
        grid=(DEPTH, 6 * D // tn),
        in_specs=[
            pl.BlockSpec((nb, D), lambda l, j: (0, 0)),
            pl.BlockSpec((1, D, tn), lambda l, j: (l, 0, j)),
            pl.BlockSpec((1, 1, tn), lambda l, j: (l, 0, j)),
        ],
        out_specs=pl.BlockSpec((1, nb, tn), lambda l, j: (l, 0, j)),
        name="adaln",
    )(c_all, w_ada, b_ada)


def _depthwise_conv(full_ref, w_ref, out_ref, *, taps, t_tile, batch):
    tc = max(1, CONV_CHUNK_ROWS // batch)
    n_chunks = t_tile // tc
    groups = tc * batch // 8
    for l0 in range(0, D, CONV_CHUNK_LANES):
        lanes = slice(l0, l0 + CONV_CHUNK_LANES)
        w = [w_ref[k, :, lanes] for k in range(taps)]

        def chunk(c, carry, lanes=lanes, w=w):
            t0 = c * tc
            acc = jnp.zeros((groups, 8, CONV_CHUNK_LANES), F32)
            for k in range(taps):
                rows = full_ref[pl.ds(t0 + k, tc), :, lanes].reshape(groups, 8, CONV_CHUNK_LANES)
                acc = acc + rows * w[k]
            out_ref[pl.ds(t0, tc), :, lanes] = acc.reshape(tc, batch, CONV_CHUNK_LANES)
            return carry

        lax.fori_loop(0, n_chunks, chunk, 0)


def _route(logits):
    m = logits[0]
    for e in range(1, N_EXPERTS):
        m = jnp.maximum(m, logits[e])
    ex = [jnp.exp(l - m) for l in logits]
    den = ex[0]
    for e in range(1, N_EXPERTS):
        den = den + ex[e]
    p = [v / den for v in ex]
    sums = []
    for g in range(N_GROUPS):
        q = p[g * EXPERTS_PER_GROUP:(g + 1) * EXPERTS_PER_GROUP]
        m1 = jnp.maximum(jnp.maximum(q[0], q[1]), jnp.maximum(q[2], q[3]))
        m2 = None
        for i in range(EXPERTS_PER_GROUP):
            for j in range(i + 1, EXPERTS_PER_GROUP):
                mn = jnp.minimum(q[i], q[j])
                m2 = mn if m2 is None else jnp.maximum(m2, mn)
        sums.append(m1 + m2)
    best = sums[0]
    grp = jnp.zeros(best.shape, I32)
    for g in range(1, N_GROUPS):
        take = sums[g] > best
        best = jnp.where(take, sums[g], best)
        grp = jnp.where(take, g, grp)
    combine, selected = [], []
    for g in range(N_GROUPS):
        q = p[g * EXPERTS_PER_GROUP:(g + 1) * EXPERTS_PER_GROUP]
        in_grp = grp == g
        for i in range(EXPERTS_PER_GROUP):
            rank = jnp.zeros(best.shape, I32)
            for j in range(EXPERTS_PER_GROUP):
                if j == i:
                    continue
                ahead = (q[j] > q[i]) if j > i else (q[j] >= q[i])
                rank = rank + ahead.astype(I32)
            sel = jnp.where(in_grp, rank, TOP_K) < TOP_K
            combine.append(jnp.where(sel, q[i] / best, 0.0))
            selected.append(jnp.where(sel, 1.0, 0.0))
    return combine, selected


def _mix_kernel(x_ref, mod_ref, h0_ref, lbuf_ref, cbuf_ref,
                gmix_ref, gffn_ref, win_ref, bin_ref,
                lcw_ref, lcb_ref, wax_ref, ba_ref, bx_ref, lam_ref, wlo_ref,
                ccw_ref, ccb_ref, lng_ref, lnb_ref, wco_ref, wout_ref,
                wr_ref, br_ref,
                x1_ref, hf_ref, route_ref, count_ref, hlast_ref, nlbuf_ref, ncbuf_ref,
                fu_ref, fg_ref, a_ref, b_ref, hcar_ref,
                *, t_tile, batch, reset):
    j = pl.program_id(1)
    n_t = pl.num_programs(1)
    rows = t_tile * batch

    @pl.when(j == 0)
    def _():
        fu_ref[0:LRU_TAPS - 1] = jnp.swapaxes(lbuf_ref[0], 0, 1)
        fg_ref[0:CONF_TAPS - 1] = jnp.swapaxes(cbuf_ref[0], 0, 1)
        hcar_ref[...] = h0_ref[0]

    x = x_ref[0]
    sh1, sc1, gt1 = mod_ref[0], mod_ref[1], mod_ref[2]
    sh2, sc2 = mod_ref[3], mod_ref[4]

    ms = jnp.mean(x * x, axis=-1, keepdims=True)
    hm = (x * lax.rsqrt(ms + RMS_EPS) * gmix_ref[...]) * (1.0 + sc1) + sh1
    hm2 = hm.reshape(rows, D).astype(BF16)

    def proj(k):
        cols = slice(k * D, (k + 1) * D)
        return jnp.dot(hm2, win_ref[:, cols], preferred_element_type=F32) + bin_ref[:, cols]

    fu_ref[LRU_TAPS - 1:] = proj(0).reshape(t_tile, batch, D)
    _depthwise_conv(fu_ref, lcw_ref, a_ref, taps=LRU_TAPS, t_tile=t_tile, batch=batch)
    xc = a_ref[...].reshape(rows, D) + lcb_ref[...]
    xcb = xc.astype(BF16)
    r_parts, i_parts = [], []
    for blk in range(N_GATE_BLOCKS):
        cols = slice(blk * GATE_BLOCK, (blk + 1) * GATE_BLOCK)
        both = jnp.dot(xcb[:, cols], wax_ref[blk], preferred_element_type=F32)
        r_parts.append(both[:, :GATE_BLOCK])
        i_parts.append(both[:, GATE_BLOCK:])
    r_gate = _sigmoid(jnp.concatenate(r_parts, axis=-1) + ba_ref[...])
    i_gate = _sigmoid(jnp.concatenate(i_parts, axis=-1) + bx_ref[...])
    neg_lam = -lam_ref[...]
    softplus = jnp.maximum(neg_lam, 0.0) + jnp.log1p(jnp.exp(-jnp.abs(neg_lam)))
    log_a = (-LRU_C) * r_gate * softplus
    a = jnp.exp(log_a)
    mult = jnp.sqrt(-jnp.tanh(log_a) * (a * a + 1.0))
    a = a.reshape(t_tile, batch, D)
    mult = mult.reshape(t_tile, batch, D)
    if reset:
        t_idx = lax.broadcasted_iota(I32, (t_tile, batch, D), 0)
        first = jnp.logical_and(t_idx == 0, j == 0)
        a = jnp.where(first, 0.0, a)
        mult = jnp.where(first, 1.0, mult)
    a_ref[...] = a
    b_ref[...] = mult * (i_gate * xc).reshape(t_tile, batch, D)

    def scan_step(t, carry):
        h = a_ref[t] * hcar_ref[...] + b_ref[t]
        hcar_ref[...] = h
        b_ref[t] = h
        return carry

    lax.fori_loop(0, t_tile, scan_step, 0)
    hseq = b_ref[...].reshape(rows, D)
    out_a = jnp.dot((hseq * _gelu_tanh(proj(1))).astype(BF16), wlo_ref[...],
                    preferred_element_type=F32)

    glu = proj(2) * _sigmoid(proj(3))
    fg_ref[CONF_TAPS - 1:] = glu.reshape(t_tile, batch, D)
    _depthwise_conv(fg_ref, ccw_ref, a_ref, taps=CONF_TAPS, t_tile=t_tile, batch=batch)
    dc = a_ref[...].reshape(rows, D) + ccb_ref[...]
    mu = jnp.mean(dc, axis=-1, keepdims=True)
    dcc = dc - mu
    var = jnp.mean(dcc * dcc, axis=-1, keepdims=True)
    ln = dcc * lax.rsqrt(var + LN_EPS) * lng_ref[...] + lnb_ref[...]
    out_b = jnp.dot(_silu(ln).astype(BF16), wco_ref[...], preferred_element_type=F32)

    merged = _sigmoid(proj(4)) * out_a + _sigmoid(proj(5)) * out_b
    mo = jnp.dot(merged.astype(BF16), wout_ref[...], preferred_element_type=F32)
    x1 = x + gt1 * mo.reshape(t_tile, batch, D)
    x1_ref[0] = x1

    ms2 = jnp.mean(x1 * x1, axis=-1, keepdims=True)
    hf = ((x1 * lax.rsqrt(ms2 + RMS_EPS) * gffn_ref[...]) * (1.0 + sc2) + sh2).reshape(rows, D)
    hf_hi = hf.astype(BF16)
    hf_ref[0] = hf_hi
    hf_lo = (hf - hf_hi.astype(F32)).astype(BF16)
    nt = (((1,), (1,)), ((), ()))
    by_hi = lax.dot_general(wr_ref[...], hf_hi, nt, preferred_element_type=F32)
    logits = (by_hi[:N_EXPERTS]
              + lax.dot_general(wr_ref[:N_EXPERTS], hf_lo, nt, preferred_element_type=F32)
              + by_hi[N_EXPERTS:]
              + br_ref[...])
    combine, selected = _route([logits[e:e + 1, :] for e in range(N_EXPERTS)])
    route_ref[0, 0] = jnp.concatenate(combine + selected, axis=0)
    n_sel = jnp.sum(jnp.concatenate(selected, axis=0), axis=-1, keepdims=True)
    count_ref[0, 0] = jnp.broadcast_to(n_sel, (N_EXPERTS, 128))

    fu_ref[0:LRU_TAPS - 1] = fu_ref[t_tile:t_tile + LRU_TAPS - 1]
    fg_ref[0:CONF_TAPS - 1] = fg_ref[t_tile:t_tile + CONF_TAPS - 1]

    @pl.when(j == n_t - 1)
    def _():
        hlast_ref[0] = hcar_ref[...]
        nlbuf_ref[...] = jnp.swapaxes(fu_ref[0:LRU_TAPS - 1], 0, 1)
        ncbuf_ref[...] = jnp.swapaxes(fg_ref[0:CONF_TAPS - 1], 0, 1)


def _mix_call(x, mod, h0, lbuf, cbuf, lw, *, reset, state_layer):
    nb, t_len, bt, _ = x.shape
    t_tile = min(t_len, MIX_ROWS // bt)
    n_t = t_len // t_tile
    rows = t_tile * bt
    kern = functools.partial(_mix_kernel, t_tile=t_tile, batch=bt, reset=reset)
    consts = [lw["g_mix"], lw["g_ffn"], lw["w_in"], lw["b_in"],
              lw["lru_conv_w"], lw["lru_conv_b"], lw["wax"], lw["lru_ba"], lw["lru_bx"],
              lw["lru_lambda"], lw["w_lru_o"], lw["conf_conv_w"], lw["conf_conv_b"],
              lw["conf_ln_g"], lw["conf_ln_b"], lw["w_conf_o"], lw["w_out"],
              lw["wr"], lw["b_router"]]
    in_specs = [
        pl.BlockSpec((1, t_tile, bt, D), lambda b, j: (b, j, 0, 0)),
        pl.BlockSpec((6, bt, D), lambda b, j: (0, b, 0)),
        pl.BlockSpec((1, bt, D), lambda b, j: (b, 0, 0)),
        pl.BlockSpec((1, bt, LRU_TAPS - 1, D), lambda b, j: (state_layer, b, 0, 0)),
        pl.BlockSpec((1, bt, CONF_TAPS - 1, D), lambda b, j: (state_layer, b, 0, 0)),
    ]
    in_specs += [_const_spec(c.shape) for c in consts]
    out_shape = (
        jax.ShapeDtypeStruct((nb, t_len, bt, D), F32),
        jax.ShapeDtypeStruct((nb * n_t, rows, D), BF16),
        jax.ShapeDtypeStruct((nb, n_t, 2 * N_EXPERTS, rows), F32),
        jax.ShapeDtypeStruct((nb, n_t, N_EXPERTS, 128), F32),
        jax.ShapeDtypeStruct((nb, bt, D), F32),
        jax.ShapeDtypeStruct((nb * bt, LRU_TAPS - 1, D), F32),
        jax.ShapeDtypeStruct((nb * bt, CONF_TAPS - 1, D), F32),
    )
    out_specs = (
        pl.BlockSpec((1, t_tile, bt, D), lambda b, j: (b, j, 0, 0)),
        pl.BlockSpec((1, rows, D), lambda b, j: (b * n_t + j, 0, 0)),
        pl.BlockSpec((1, 1, 2 * N_EXPERTS, rows), lambda b, j: (b, j, 0, 0)),
        pl.BlockSpec((1, 1, N_EXPERTS, 128), lambda b, j: (b, j, 0, 0)),
        pl.BlockSpec((1, bt, D), lambda b, j: (b, 0, 0)),
        pl.BlockSpec((bt, LRU_TAPS - 1, D), lambda b, j: (b, 0, 0)),
        pl.BlockSpec((bt, CONF_TAPS - 1, D), lambda b, j: (b, 0, 0)),
    )
    scratch = [
        pltpu.VMEM((t_tile + LRU_TAPS - 1, bt, D), F32),
        pltpu.VMEM((t_tile + CONF_TAPS - 1, bt, D), F32),
        pltpu.VMEM((t_tile, bt, D), F32),
        pltpu.VMEM((t_tile, bt, D), F32),
        pltpu.VMEM((bt, D), F32),
    ]
    return pl.pallas_call(
        kern,
        out_shape=out_shape,
        grid=(nb, n_t),
        in_specs=in_specs,
        out_specs=out_specs,
        scratch_shapes=scratch,
        compiler_params=pltpu.CompilerParams(
            dimension_semantics=("arbitrary", "arbitrary"), vmem_limit_bytes=VMEM_LIMIT),
        name="mix_reset" if reset else "mix_cont",
    )(x, mod, h0, lbuf, cbuf, *consts)


def _segment_issue(seg_len_ref, seg_start_ref, tile, make_copy):
    off = 0
    for e in range(N_EXPERTS):
        length = seg_len_ref[tile * N_EXPERTS + e]
        start = seg_start_ref[tile * N_EXPERTS + e]
        n_big = length // SEG_BLOCK
        done = n_big * SEG_BLOCK

        def issue_big(k, carry, off=off, start=start):
            local = pl.multiple_of(off + k * SEG_BLOCK, SEG_ALIGN)
            packed = pl.multiple_of(start + k * SEG_BLOCK, SEG_ALIGN)
            make_copy(local, packed, SEG_BLOCK).start()
            return carry

        def issue_small(k, carry, off=off + done, start=start + done):
            local = pl.multiple_of(off + k * SEG_ALIGN, SEG_ALIGN)
            packed = pl.multiple_of(start + k * SEG_ALIGN, SEG_ALIGN)
            make_copy(local, packed, SEG_ALIGN).start()
            return carry

        lax.fori_loop(0, n_big, issue_big, 0)
        lax.fori_loop(0, (length - done) // SEG_ALIGN, issue_small, 0)
        off = off + length


def _segment_wait(seg_len_ref, tile, make_copy):
    n_big, n_small = 0, 0
    for e in range(N_EXPERTS):
        length = seg_len_ref[tile * N_EXPERTS + e]
        n_big = n_big + length // SEG_BLOCK
        n_small = n_small + (length % SEG_BLOCK) // SEG_ALIGN

    def wait_big(k, carry):
        make_copy(0, 0, SEG_BLOCK).wait()
        return carry

    def wait_small(k, carry):
        make_copy(0, 0, SEG_ALIGN).wait()
        return carry

    lax.fori_loop(0, n_big, wait_big, 0)
    lax.fori_loop(0, n_small, wait_small, 0)


def _dispatch_kernel(seg_len_ref, seg_start_ref, tail_ref, *refs, chunk, group_tiles):
    hf_refs = refs[:len(group_tiles)]
    route_ref, xbuf_ref, meta_ref, xs_ref, zero_ref, sem = refs[len(group_tiles):]
    i = pl.program_id(0)
    last = pl.num_programs(0) - 1
    comb = route_ref[0, 0:N_EXPERTS, :]
    sel = route_ref[0, N_EXPERTS:2 * N_EXPERTS, :]
    on = sel > 0.0

    r_idx = lax.broadcasted_iota(I32, (MOE_TILE, MOE_TILE), 0)
    c_idx = lax.broadcasted_iota(I32, (MOE_TILE, MOE_TILE), 1)
    upper = jnp.where(r_idx <= c_idx, 1.0, 0.0).astype(BF16)
    csum = jnp.dot(sel.astype(BF16), upper, preferred_element_type=F32)
    rank = csum - sel
    count = csum[:, MOE_TILE - 1:MOE_TILE].astype(I32)
    padded = ((count + (SEG_ALIGN - 1)) // SEG_ALIGN) * SEG_ALIGN
    e_r = lax.broadcasted_iota(I32, (N_EXPERTS, N_EXPERTS), 0)
    e_c = lax.broadcasted_iota(I32, (N_EXPERTS, N_EXPERTS), 1)
    lower = jnp.where(e_c < e_r, 1.0, 0.0).astype(BF16)
    padded_b = jnp.broadcast_to(padded.astype(F32), (N_EXPERTS, 128)).astype(BF16)
    seg_off = jnp.dot(lower, padded_b, preferred_element_type=F32)[:, 0:1]

    pos = seg_off + rank
    p_lo = jnp.min(jnp.where(on, pos, 4.0 * SORTED_ROWS), axis=0, keepdims=True)
    p_hi = jnp.max(jnp.where(on, pos, -1.0), axis=0, keepdims=True)
    w_lo = jnp.sum(jnp.where(jnp.logical_and(on, pos == p_lo), comb, 0.0), axis=0, keepdims=True)
    w_hi = jnp.sum(jnp.where(jnp.logical_and(on, pos == p_hi), comb, 0.0), axis=0, keepdims=True)
    meta_ref[0] = jnp.concatenate(
        [p_lo, p_hi, w_lo, w_hi, jnp.zeros((4, MOE_TILE), F32)], axis=0)

    s_idx = lax.broadcasted_iota(I32, (SORTED_ROWS, MOE_TILE), 0)
    one_hot = jnp.where(s_idx == p_lo.astype(I32), 1.0,
                        jnp.where(s_idx == p_hi.astype(I32), 1.0, 0.0)).astype(BF16)
    hf = hf_refs[0][...]
    first_tile = group_tiles[0]
    for g in range(1, len(group_tiles)):
        hf = jnp.where(i >= first_tile, hf_refs[g][...], hf)
        first_tile += group_tiles[g]
    sorted_rows = jnp.dot(one_hot, hf, preferred_element_type=F32).astype(BF16)

    def seg_copy(local, packed, n_rows):
        return pltpu.make_async_copy(xs_ref.at[pl.ds(local, n_rows), :],
                                     xbuf_ref.at[pl.ds(packed, n_rows), :], sem.at[0])

    @pl.when(i > 0)
    def _():
        _segment_wait(seg_len_ref, i - 1, seg_copy)

    xs_ref[...] = sorted_rows
    _segment_issue(seg_len_ref, seg_start_ref, i, seg_copy)

    @pl.when(i == last)
    def _():
        _segment_wait(seg_len_ref, i, seg_copy)
        zero_ref[...] = jnp.zeros_like(zero_ref)

        def pad_copy(row):
            return pltpu.make_async_copy(zero_ref.at[pl.ds(0, SEG_ALIGN), :],
                                         xbuf_ref.at[pl.ds(row, SEG_ALIGN), :], sem.at[1])

        n_pad = 0
        for e in range(N_EXPERTS):
            def issue(k, carry, e=e):
                pad_copy(pl.multiple_of(tail_ref[e] + k * SEG_ALIGN, SEG_ALIGN)).start()
                return carry

            lax.fori_loop(0, tail_ref[N_EXPERTS + e], issue, 0)
            n_pad = n_pad + tail_ref[N_EXPERTS + e]

        def wait_pad(k, carry):
            pad_copy(0).wait()
            return carry

        lax.fori_loop(0, n_pad, wait_pad, 0)

        def idle_copy(c):
            return pltpu.make_async_copy(
                zero_ref, xbuf_ref.at[pl.ds(pl.multiple_of(c * chunk, chunk), chunk), :], sem.at[1])

        def issue_idle(c, carry):
            idle_copy(c).start()
            return carry

        def wait_idle(c, carry):
            idle_copy(c).wait()
            return carry

        n_chunks = xbuf_ref.shape[0] // chunk
        lax.fori_loop(tail_ref[2 * N_EXPERTS], n_chunks, issue_idle, 0)
        lax.fori_loop(tail_ref[2 * N_EXPERTS], n_chunks, wait_idle, 0)


def _dispatch_call(hf_groups, route, seg_len, seg_start, tail, *, chunk, n_chunks):
    n_tiles = route.shape[0]
    group_tiles = tuple(hf.shape[0] // MOE_TILE for hf in hf_groups)
    first_tiles = [sum(group_tiles[:g]) for g in range(len(group_tiles))]

    def hf_spec(first, count):
        return pl.BlockSpec((MOE_TILE, D), lambda i, *_: (jnp.clip(i - first, 0, count - 1), 0))

    grid_spec = pltpu.PrefetchScalarGridSpec(
        num_scalar_prefetch=3,
        grid=(n_tiles,),
        in_specs=[hf_spec(f, c) for f, c in zip(first_tiles, group_tiles)] + [
            pl.BlockSpec((1, 2 * N_EXPERTS, MOE_TILE), lambda i, *_: (i, 0, 0)),
        ],
        out_specs=(
            pl.BlockSpec(memory_space=pl.ANY),
            pl.BlockSpec((1, 8, MOE_TILE), lambda i, *_: (i, 0, 0)),
        ),
        scratch_shapes=[
            pltpu.VMEM((SORTED_ROWS, D), BF16),
            pltpu.VMEM((chunk, D), BF16),
            pltpu.SemaphoreType.DMA((2,)),
        ],
    )
    return pl.pallas_call(
        functools.partial(_dispatch_kernel, chunk=chunk, group_tiles=group_tiles),
        out_shape=(jax.ShapeDtypeStruct((n_chunks * chunk, D), BF16),
                   jax.ShapeDtypeStruct((n_tiles, 8, MOE_TILE), F32)),
        grid_spec=grid_spec,
        compiler_params=pltpu.CompilerParams(
            dimension_semantics=("arbitrary",), vmem_limit_bytes=VMEM_LIMIT),
        name="moe_dispatch",
    )(seg_len, seg_start, tail, *hf_groups, route)


def _expert_kernel(e_ref, x_idx_ref, valid_ref, x_ref, wg_ref, wu_ref, wd_ref, y_ref,
                   wg_bf, wu_bf, wd_bf):
    s = pl.program_id(0)

    @pl.when(jnp.logical_or(s == 0, e_ref[s] != e_ref[jnp.maximum(s - 1, 0)]))
    def _():
        wg_bf[...] = wg_ref[0, 0].astype(BF16)
        wu_bf[...] = wu_ref[0, 0].astype(BF16)
        wd_bf[...] = wd_ref[0, 0].astype(BF16)

    @pl.when(valid_ref[s] == 1)
    def _():
        x = x_ref[...]
        g = jnp.dot(x, wg_bf[...], preferred_element_type=F32)
        u = jnp.dot(x, wu_bf[...], preferred_element_type=F32)
        y = jnp.dot((_silu(g) * u).astype(BF16), wd_bf[...], preferred_element_type=F32)
        y_ref[...] = y.astype(BF16)

    @pl.when(valid_ref[s] == 0)
    def _():
        y_ref[...] = jnp.zeros_like(y_ref)


def _expert_call(xbuf, e_tab, x_idx_tab, valid_tab, lw, *, chunk):
    n_steps = e_tab.shape[0]
    layer = lw["layer"]
    grid_spec = pltpu.PrefetchScalarGridSpec(
        num_scalar_prefetch=3,
        grid=(n_steps,),
        in_specs=[
            pl.BlockSpec((chunk, D), lambda s, e, x, v: (x[s], 0)),
            pl.BlockSpec((1, 1, D, D_EXPERT), lambda s, e, x, v: (layer, e[s], 0, 0)),
            pl.BlockSpec((1, 1, D, D_EXPERT), lambda s, e, x, v: (layer, e[s], 0, 0)),
            pl.BlockSpec((1, 1, D_EXPERT, D), lambda s, e, x, v: (layer, e[s], 0, 0)),
        ],
        out_specs=pl.BlockSpec((chunk, D), lambda s, e, x, v: (s, 0)),
        scratch_shapes=[
            pltpu.VMEM((D, D_EXPERT), BF16),
            pltpu.VMEM((D, D_EXPERT), BF16),
            pltpu.VMEM((D_EXPERT, D), BF16),
        ],
    )
    return pl.pallas_call(
        _expert_kernel,
        out_shape=jax.ShapeDtypeStruct(xbuf.shape, BF16),
        grid_spec=grid_spec,
        compiler_params=pltpu.CompilerParams(
            dimension_semantics=("arbitrary",), vmem_limit_bytes=VMEM_LIMIT),
        name="moe_experts",
    )(e_tab, x_idx_tab, valid_tab, xbuf, lw["w_gate"], lw["w_up"], lw["w_down"])


def _combine_kernel(seg_len_ref, seg_start_ref, ybuf_ref, meta_ref, x1_ref, mod_ref, gfin_ref,
                    o_ref, ys_ref, sem, *, final, first_tile):
    i = pl.program_id(0)
    n_tiles = pl.num_programs(0)
    slot = i % 2
    tile = first_tile + i

    def seg_copy(buf):
        def make(local, packed, n_rows):
            return pltpu.make_async_copy(ybuf_ref.at[pl.ds(packed, n_rows), :],
                                         ys_ref.at[buf, pl.ds(local, n_rows), :], sem.at[buf])
        return make

    @pl.when(i == 0)
    def _():
        ys_ref[...] = jnp.zeros_like(ys_ref)
        _segment_issue(seg_len_ref, seg_start_ref, tile, seg_copy(0))

    @pl.when(i + 1 < n_tiles)
    def _():
        _segment_issue(seg_len_ref, seg_start_ref, tile + 1, seg_copy(1 - slot))

    _segment_wait(seg_len_ref, tile, seg_copy(slot))

    meta = meta_ref[0]
    p_lo = meta[:, 0:1].astype(I32)
    p_hi = meta[:, 1:2].astype(I32)
    w_lo = meta[:, 2:3]
    w_hi = meta[:, 3:4]
    s_idx = lax.broadcasted_iota(I32, (MOE_TILE, SORTED_ROWS), 1)
    ys = ys_ref[slot]
    y_lo = jnp.dot(jnp.where(s_idx == p_lo, 1.0, 0.0).astype(BF16), ys, preferred_element_type=F32)
    y_hi = jnp.dot(jnp.where(s_idx == p_hi, 1.0, 0.0).astype(BF16), ys, preferred_element_type=F32)
    moe = w_lo * y_lo + w_hi * y_hi

    nbt, t_tile, bt, _ = x1_ref.shape
    gt2 = mod_ref[5].reshape(nbt, 1, bt, D)
    x2 = x1_ref[...] + gt2 * moe.reshape(nbt, t_tile, bt, D)
    if final:
        ms = jnp.mean(x2 * x2, axis=-1, keepdims=True)
        x2 = x2 * lax.rsqrt(ms + RMS_EPS) * gfin_ref[...]
    o_ref[...] = x2


def _combine_call(ybuf, meta_t, x1, mod, g_final, seg_len, seg_start, *, final, first_tile):
    nb, t_len, bt, _ = x1.shape
    if nb * t_len * bt == MOE_TILE:
        nbt, t_tile = nb, t_len
    else:
        nbt, t_tile = 1, MOE_TILE // bt
    n_t = t_len // t_tile
    grid_spec = pltpu.PrefetchScalarGridSpec(
        num_scalar_prefetch=2,
        grid=((nb // nbt) * n_t,),
        in_specs=[
            pl.BlockSpec(memory_space=pl.ANY),
            pl.BlockSpec((1, MOE_TILE, 8), lambda i, *_: (i, 0, 0)),
            pl.BlockSpec((nbt, t_tile, bt, D), lambda i, *_: (i // n_t, i % n_t, 0, 0)),
            pl.BlockSpec((6, nbt * bt, D), lambda i, *_: (0, i // n_t, 0)),
            pl.BlockSpec((1, D), lambda i, *_: (0, 0)),
        ],
        out_specs=pl.BlockSpec((nbt, t_tile, bt, D), lambda i, *_: (i // n_t, i % n_t, 0, 0)),
        scratch_shapes=[
            pltpu.VMEM((2, SORTED_ROWS, D), BF16),
            pltpu.SemaphoreType.DMA((2,)),
        ],
    )
    return pl.pallas_call(
        functools.partial(_combine_kernel, final=final, first_tile=first_tile),
        out_shape=jax.ShapeDtypeStruct((nb, t_len, bt, D), F32),
        grid_spec=grid_spec,
        compiler_params=pltpu.CompilerParams(
            dimension_semantics=("arbitrary",), vmem_limit_bytes=VMEM_LIMIT),
        name="moe_combine_final" if final else "moe_combine",
    )(seg_len, seg_start, ybuf, meta_t, x1, mod, g_final)


def _moe_sparse(groups, lw, g_final, *, final):
    hf_groups = [g[0] for g in groups]
    group_tiles = [hf.shape[0] // MOE_TILE for hf in hf_groups]
    n_tok = sum(hf.shape[0] for hf in hf_groups)
    n_tiles = sum(group_tiles)
    def per_tile(r):
        if r.shape[-1] == MOE_TILE:
            return r.reshape(-1, 2 * N_EXPERTS, MOE_TILE)
        return r.transpose(0, 1, 3, 2).reshape(-1, MOE_TILE, 2 * N_EXPERTS).transpose(0, 2, 1)

    route = jnp.concatenate([per_tile(g[1]) for g in groups])

    count = jnp.concatenate([
        g[2][..., 0].astype(I32).reshape(t, -1, N_EXPERTS).sum(axis=1)
        for g, t in zip(groups, group_tiles)])
    chunk = EXPERT_CHUNK if n_tok >= N_EXPERTS * EXPERT_CHUNK else SMALL_EXPERT_CHUNK
    seg_len = (count + SEG_ALIGN - 1) // SEG_ALIGN * SEG_ALIGN
    total = jnp.sum(seg_len, axis=0)
    n_chunks = (total + chunk - 1) // chunk
    chunk_end = jnp.cumsum(n_chunks)
    used = chunk_end[-1]
    base = (chunk_end - n_chunks) * chunk
    seg_start = base[None, :] + jnp.cumsum(seg_len, axis=0) - seg_len
    max_chunks = -(-(TOP_K * n_tok + N_EXPERTS * SEG_ALIGN * n_tiles) // chunk) + N_EXPERTS
    step = jnp.arange(max_chunks, dtype=I32)
    valid = step < used
    x_idx = jnp.minimum(step, used - 1)
    e_tab = jnp.minimum(jnp.sum(x_idx[:, None] >= chunk_end[None, :], axis=1), N_EXPERTS - 1)
    tail = jnp.concatenate([base + total, (n_chunks * chunk - total) // SEG_ALIGN, used[None]])

    as_i32 = lambda v: v.reshape(-1).astype(I32)
    xbuf, meta = _dispatch_call(hf_groups, route, as_i32(seg_len), as_i32(seg_start), as_i32(tail),
                                chunk=chunk, n_chunks=max_chunks)
    ybuf = _expert_call(xbuf, as_i32(e_tab), as_i32(x_idx), as_i32(valid), lw, chunk=chunk)
    meta_t = meta.transpose(0, 2, 1)
    outs, first_tile = [], 0
    for (_, _, _, x1, mod), tiles in zip(groups, group_tiles):
        outs.append(_combine_call(ybuf, meta_t[first_tile:first_tile + tiles], x1, mod, g_final,
                                  as_i32(seg_len), as_i32(seg_start), final=final,
                                  first_tile=first_tile))
        first_tile += tiles
    return outs


def _block_diag(w):
    heads_per_block = GATE_BLOCK // w.shape[-1]
    w = w.reshape(N_GATE_BLOCKS, heads_per_block, w.shape[-2], w.shape[-1])
    eye = jnp.eye(heads_per_block, dtype=w.dtype)
    out = jnp.einsum("ghij,hk->ghikj", w, eye)
    return out.reshape(N_GATE_BLOCKS, GATE_BLOCK, GATE_BLOCK).astype(BF16)


def _to_tiled(x, bt):
    b, t, d = x.shape
    return x.reshape(b // bt, bt, t, d).transpose(0, 2, 1, 3)


def _from_tiled(x):
    nb, t, bt, d = x.shape
    return x.transpose(0, 2, 1, 3).reshape(nb * bt, t, d)


def kernel(x_prompt, x_sample, state_lru_h, state_lru_conv, state_conf_conv, c_prompt, c_sample, w_ada, b_ada, g_mix, g_ffn, w_in, b_in, lru_conv_w, lru_conv_b, lru_wa, lru_ba, lru_wx, lru_bx, lru_lambda, w_lru_o, conf_conv_w, conf_conv_b, conf_ln_g, conf_ln_b, w_conf_o, w_out, w_router, b_router, w_gate, w_up, w_down, g_final):
    bp = x_prompt.shape[0]
    bs = x_sample.shape[0]
    btp = min(bp, MAX_BATCH_TILE)
    bts = min(bs, MAX_BATCH_TILE)
    row = lambda v: v.reshape(1, -1)
    sub8 = lambda w: jnp.broadcast_to(w[:, None, :], (w.shape[0], 8, w.shape[1]))

    c_all = jnp.concatenate([c_prompt, c_sample], axis=0)
    mod = _ada_call(c_all, w_ada, b_ada.reshape(DEPTH, 1, 6 * D))
    mod = mod.reshape(DEPTH, bp + bs, 6, D).transpose(0, 2, 1, 3)

    wr_t = w_router.T
    wr_hi = wr_t.astype(BF16)
    wr_lo = (wr_t - wr_hi.astype(F32)).astype(BF16)
    wr = jnp.concatenate([wr_hi, wr_lo], axis=0)
    g_fin = row(g_final)

    xp = _to_tiled(x_prompt, btp)
    xs = _to_tiled(x_sample, bts)

    outs_p, outs_s = [], []
    for l in range(DEPTH):
        lw = {
            "g_mix": row(g_mix[l]), "g_ffn": row(g_ffn[l]),
            "w_in": w_in[l].astype(BF16), "b_in": row(b_in[l]),
            "lru_conv_w": sub8(lru_conv_w[l]), "lru_conv_b": row(lru_conv_b[l]),
            "wax": jnp.concatenate([_block_diag(lru_wa[l]), _block_diag(lru_wx[l])], axis=-1),
            "lru_ba": row(lru_ba[l]), "lru_bx": row(lru_bx[l]),
            "lru_lambda": row(lru_lambda[l]), "w_lru_o": w_lru_o[l].astype(BF16),
            "conf_conv_w": sub8(conf_conv_w[l]), "conf_conv_b": row(conf_conv_b[l]),
            "conf_ln_g": row(conf_ln_g[l]), "conf_ln_b": row(conf_ln_b[l]),
            "w_conf_o": w_conf_o[l].astype(BF16), "w_out": w_out[l].astype(BF16),
            "wr": wr, "b_router": b_router.reshape(N_EXPERTS, 1),
            "w_gate": w_gate, "w_up": w_up, "w_down": w_down, "layer": l,
        }
        final = l == DEPTH - 1
        groups = (
            (xp, mod[l, :, :bp], jnp.zeros((bp // btp, btp, D), F32),
             jnp.zeros((1, bp, LRU_TAPS - 1, D), F32),
             jnp.zeros((1, bp, CONF_TAPS - 1, D), F32), True, 0),
            (xs, mod[l, :, bp:], state_lru_h[l].reshape(bs // bts, bts, D),
             state_lru_conv, state_conf_conv, False, l),
        )
        moe_groups = []
        for (xg, modg, h0, lbuf, cbuf, reset, state_layer), outs in zip(groups, (outs_p, outs_s)):
            x1, hf, route, count, hlast, nlbuf, ncbuf = _mix_call(
                xg, modg, h0, lbuf, cbuf, lw, reset=reset, state_layer=state_layer)
            moe_groups.append((hf.reshape(-1, D), route, count, x1, modg))
            outs.append((hlast.reshape(-1, D), nlbuf, ncbuf))
        xp, xs = _moe_sparse(moe_groups, lw, g_fin, final=final)

    stack = lambda outs, k: jnp.stack([o[k] for o in outs])
    return (_from_tiled(xp), _from_tiled(xs),
            stack(outs_p, 0), stack(outs_p, 1), stack(outs_p, 2),
            stack(outs_s, 0), stack(outs_s, 1), stack(outs_s, 2))
```

```python
import functools

import jax
import jax.numpy as jnp
from jax import lax
from jax.experimental import pallas as pl
from jax.experimental.pallas import tpu as pltpu

F32 = jnp.float32
BF16 = jnp.bfloat16
I32 = jnp.int32

D = 1024
DEPTH = 2
N_EXPERTS = 16
EXPERTS_PER_GROUP = 4
N_GROUPS = N_EXPERTS // EXPERTS_PER_GROUP
TOP_K = 2
D_EXPERT = D // 2
LRU_TAPS = 4
CONF_TAPS = 31
LRU_C = 8.0
RMS_EPS = 1e-6
LN_EPS = 1e-5
GATE_BLOCK = 256
N_GATE_BLOCKS = D // GATE_BLOCK
MIX_ROWS = 512
MAX_BATCH_TILE = 32
CONV_CHUNK_ROWS = 128
CONV_CHUNK_LANES = 128
MOE_TILE = 512
SEG_ALIGN = 16
SEG_BLOCK = 64
SORTED_ROWS = TOP_K * MOE_TILE + N_EXPERTS * SEG_ALIGN
EXPERT_CHUNK = 512
SMALL_EXPERT_CHUNK = 128
VMEM_LIMIT = 56 * 1024 * 1024


def _sigmoid(x):
    return 1.0 / (1.0 + jnp.exp(-x))


def _silu(x):
    return x * _sigmoid(x)


def _gelu_tanh(x):
    c = 0.7978845608028654
    return 0.5 * x * (1.0 + jnp.tanh(c * (x + 0.044715 * (x * x * x))))


def _const_spec(shape):
    zeros = (0,) * len(shape)
    return pl.BlockSpec(shape, lambda *_: zeros, pipeline_mode=pl.Buffered(1))


def _ada_kernel(c_ref, w_ref, b_ref, o_ref):
    s = _silu(c_ref[...]).astype(BF16)
    o_ref[0] = jnp.dot(s, w_ref[0].astype(BF16), preferred_element_type=F32) + b_ref[0]


def _ada_call(c_all, w_ada, b_ada):
    nb = c_all.shape[0]
    tn = 1536
    return pl.pallas_call(
        _ada_kernel,
        out_shape=jax.ShapeDtypeStruct((DEPTH, nb, 6 * D), F32),
        grid=(DEPTH, 6 * D // tn),
        in_specs=[
            pl.BlockSpec((nb, D), lambda l, j: (0, 0)),
            pl.BlockSpec((1, D, tn), lambda l, j: (l, 0, j)),
            pl.BlockSpec((1, 1, tn), lambda l, j: (l, 0, j)),
        ],
        out_specs=pl.BlockSpec((1, nb, tn), lambda l, j: (l, 0, j)),
        name="adaln",
    )(c_all, w_ada, b_ada)


def _depthwise_conv(full_ref, w_ref, out_ref, *, taps, t_tile, batch):
    tc = max(1, CONV_CHUNK_ROWS // batch)
    n_chunks = t_tile // tc
    groups = tc * batch // 8
    for l0 in range(0, D, CONV_CHUNK_LANES):
        lanes = slice(l0, l0 + CONV_CHUNK_LANES)
        w = [w_ref[k, :, lanes] for k in range(taps)]

        def chunk(c, carry, lanes=lanes, w=w):
            t0 = c * tc
            acc = jnp.zeros((groups, 8, CONV_CHUNK_LANES), F32)
            for k in range(taps):
                rows = full_ref[pl.ds(t0 + k, tc), :, lanes].reshape(groups, 8, CONV_CHUNK_LANES)
                acc = acc + rows * w[k]
            out_ref[pl.ds(t0, tc), :, lanes] = acc.reshape(tc, batch, CONV_CHUNK_LANES)
            return carry

        lax.fori_loop(0, n_chunks, chunk, 0)


def _route(logits):
    m = logits[0]
    for e in range(1, N_EXPERTS):
        m = jnp.maximum(m, logits[e])
    ex = [jnp.exp(l - m) for l in logits]
    den = ex[0]
    for e in range(1, N_EXPERTS):
        den = den + ex[e]
    p = [v / den for v in ex]
    sums = []
    for g in range(N_GROUPS):
        q = p[g * EXPERTS_PER_GROUP:(g + 1) * EXPERTS_PER_GROUP]
        m1 = jnp.maximum(jnp.maximum(q[0], q[1]), jnp.maximum(q[2], q[3]))
        m2 = None
        for i in range(EXPERTS_PER_GROUP):
            for j in range(i + 1, EXPERTS_PER_GROUP):
                mn = jnp.minimum(q[i], q[j])
                m2 = mn if m2 is None else jnp.maximum(m2, mn)
        sums.append(m1 + m2)
    best = sums[0]
    grp = jnp.zeros(best.shape, I32)
    for g in range(1, N_GROUPS):
        take = sums[g] > best
        best = jnp.where(take, sums[g], best)
        grp = jnp.where(take, g, grp)
    combine, selected = [], []
    for g in range(N_GROUPS):
        q = p[g * EXPERTS_PER_GROUP:(g + 1) * EXPERTS_PER_GROUP]
        in_grp = grp == g
        for i in range(EXPERTS_PER_GROUP):
            rank = jnp.zeros(best.shape, I32)
            for j in range(EXPERTS_PER_GROUP):
                if j == i:
                    continue
                ahead = (q[j] > q[i]) if j > i else (q[j] >= q[i])
                rank = rank + ahead.astype(I32)
            sel = jnp.where(in_grp, rank, TOP_K) < TOP_K
            combine.append(jnp.where(sel, q[i] / best, 0.0))
            selected.append(jnp.where(sel, 1.0, 0.0))
    return combine, selected


def _mix_kernel(x_ref, mod_ref, h0_ref, lbuf_ref, cbuf_ref,
                gmix_ref, gffn_ref, win_ref, bin_ref,
                lcw_ref, lcb_ref, wax_ref, ba_ref, bx_ref, lam_ref, wlo_ref,
                ccw_ref, ccb_ref, lng_ref, lnb_ref, wco_ref, wout_ref,
                wr_ref, br_ref,
                x1_ref, hf_ref, route_ref, count_ref, hlast_ref, nlbuf_ref, ncbuf_ref,
                fu_ref, fg_ref, a_ref, b_ref, hcar_ref,
                *, t_tile, batch, reset, batch_major_in):
    j = pl.program_id(1)
    n_t = pl.num_programs(1)
    rows = t_tile * batch

    @pl.when(j == 0)
    def _():
        fu_ref[0:LRU_TAPS - 1] = jnp.swapaxes(lbuf_ref[0], 0, 1)
        fg_ref[0:CONF_TAPS - 1] = jnp.swapaxes(cbuf_ref[0], 0, 1)
        hcar_ref[...] = h0_ref[0]

    if batch_major_in:
        x = jnp.swapaxes(x_ref[...], 0, 1)
    else:
        x = x_ref[0]
    sh1, sc1, gt1 = mod_ref[0], mod_ref[1], mod_ref[2]
    sh2, sc2 = mod_ref[3], mod_ref[4]

    ms = jnp.mean(x * x, axis=-1, keepdims=True)
    hm = (x * lax.rsqrt(ms + RMS_EPS) * gmix_ref[...]) * (1.0 + sc1) + sh1
    hm2 = hm.reshape(rows, D).astype(BF16)

    def proj(k):
        cols = slice(k * D, (k + 1) * D)
        return jnp.dot(hm2, win_ref[:, cols], preferred_element_type=F32) + bin_ref[:, cols]

    fu_ref[LRU_TAPS - 1:] = proj(0).reshape(t_tile, batch, D)
    _depthwise_conv(fu_ref, lcw_ref, a_ref, taps=LRU_TAPS, t_tile=t_tile, batch=batch)
    xc = a_ref[...].reshape(rows, D) + lcb_ref[...]
    xcb = xc.astype(BF16)
    r_parts, i_parts = [], []
    for blk in range(N_GATE_BLOCKS):
        cols = slice(blk * GATE_BLOCK, (blk + 1) * GATE_BLOCK)
        both = jnp.dot(xcb[:, cols], wax_ref[blk], preferred_element_type=F32)
        r_parts.append(both[:, :GATE_BLOCK])
        i_parts.append(both[:, GATE_BLOCK:])
    r_gate = _sigmoid(jnp.concatenate(r_parts, axis=-1) + ba_ref[...])
    i_gate = _sigmoid(jnp.concatenate(i_parts, axis=-1) + bx_ref[...])
    neg_lam = -lam_ref[...]
    softplus = jnp.maximum(neg_lam, 0.0) + jnp.log1p(jnp.exp(-jnp.abs(neg_lam)))
    log_a = (-LRU_C) * r_gate * softplus
    a = jnp.exp(log_a)
    mult = jnp.sqrt(-jnp.tanh(log_a) * (a * a + 1.0))
    a = a.reshape(t_tile, batch, D)
    mult = mult.reshape(t_tile, batch, D)
    if reset:
        t_idx = lax.broadcasted_iota(I32, (t_tile, batch, D), 0)
        first = jnp.logical_and(t_idx == 0, j == 0)
        a = jnp.where(first, 0.0, a)
        mult = jnp.where(first, 1.0, mult)
    a_ref[...] = a
    b_ref[...] = mult * (i_gate * xc).reshape(t_tile, batch, D)

    def scan_step(t, carry):
        h = a_ref[t] * hcar_ref[...] + b_ref[t]
        hcar_ref[...] = h
        b_ref[t] = h
        return carry

    lax.fori_loop(0, t_tile, scan_step, 0)
    hseq = b_ref[...].reshape(rows, D)
    out_a = jnp.dot((hseq * _gelu_tanh(proj(1))).astype(BF16), wlo_ref[...],
                    preferred_element_type=F32)

    glu = proj(2) * _sigmoid(proj(3))
    fg_ref[CONF_TAPS - 1:] = glu.reshape(t_tile, batch, D)
    _depthwise_conv(fg_ref, ccw_ref, a_ref, taps=CONF_TAPS, t_tile=t_tile, batch=batch)
    dc = a_ref[...].reshape(rows, D) + ccb_ref[...]
    mu = jnp.mean(dc, axis=-1, keepdims=True)
    dcc = dc - mu
    var = jnp.mean(dcc * dcc, axis=-1, keepdims=True)
    ln = dcc * lax.rsqrt(var + LN_EPS) * lng_ref[...] + lnb_ref[...]
    out_b = jnp.dot(_silu(ln).astype(BF16), wco_ref[...], preferred_element_type=F32)

    merged = _sigmoid(proj(4)) * out_a + _sigmoid(proj(5)) * out_b
    mo = jnp.dot(merged.astype(BF16), wout_ref[...], preferred_element_type=F32)
    x1 = x + gt1 * mo.reshape(t_tile, batch, D)
    x1_ref[0] = x1

    ms2 = jnp.mean(x1 * x1, axis=-1, keepdims=True)
    hf = ((x1 * lax.rsqrt(ms2 + RMS_EPS) * gffn_ref[...]) * (1.0 + sc2) + sh2).reshape(rows, D)
    hf_hi = hf.astype(BF16)
    hf_ref[0] = hf_hi
    hf_lo = (hf - hf_hi.astype(F32)).astype(BF16)
    nt = (((1,), (1,)), ((), ()))
    by_hi = lax.dot_general(wr_ref[...], hf_hi, nt, preferred_element_type=F32)
    logits = (by_hi[:N_EXPERTS]
              + lax.dot_general(wr_ref[:N_EXPERTS], hf_lo, nt, preferred_element_type=F32)
              + by_hi[N_EXPERTS:]
              + br_ref[...])
    combine, selected = _route([logits[e:e + 1, :] for e in range(N_EXPERTS)])
    route_ref[0, 0] = jnp.concatenate(combine + selected, axis=0)
    n_sel = jnp.sum(jnp.concatenate(selected, axis=0), axis=-1, keepdims=True)
    count_ref[0, 0] = jnp.broadcast_to(n_sel, (N_EXPERTS, 128))

    fu_ref[0:LRU_TAPS - 1] = fu_ref[t_tile:t_tile + LRU_TAPS - 1]
    fg_ref[0:CONF_TAPS - 1] = fg_ref[t_tile:t_tile + CONF_TAPS - 1]

    @pl.when(j == n_t - 1)
    def _():
        hlast_ref[0] = hcar_ref[...]
        nlbuf_ref[...] = jnp.swapaxes(fu_ref[0:LRU_TAPS - 1], 0, 1)
        ncbuf_ref[...] = jnp.swapaxes(fg_ref[0:CONF_TAPS - 1], 0, 1)


def _mix_call(x, mod, h0, lbuf, cbuf, lw, *, reset, state_layer):
    batch_major_in = x.ndim == 3
    if batch_major_in:
        (bt, t_len, _), nb = x.shape, 1
    else:
        nb, t_len, bt, _ = x.shape
    t_tile = min(t_len, MIX_ROWS // bt)
    n_t = t_len // t_tile
    rows = t_tile * bt
    kern = functools.partial(_mix_kernel, t_tile=t_tile, batch=bt, reset=reset,
                             batch_major_in=batch_major_in)
    x_spec = (pl.BlockSpec((bt, t_tile, D), lambda b, j: (0, j, 0)) if batch_major_in else
              pl.BlockSpec((1, t_tile, bt, D), lambda b, j: (b, j, 0, 0)))
    consts = [lw["g_mix"], lw["g_ffn"], lw["w_in"], lw["b_in"],
              lw["lru_conv_w"], lw["lru_conv_b"], lw["wax"], lw["lru_ba"], lw["lru_bx"],
              lw["lru_lambda"], lw["w_lru_o"], lw["conf_conv_w"], lw["conf_conv_b"],
              lw["conf_ln_g"], lw["conf_ln_b"], lw["w_conf_o"], lw["w_out"],
              lw["wr"], lw["b_router"]]
    in_specs = [
        x_spec,
        pl.BlockSpec((6, bt, D), lambda b, j: (0, b, 0)),
        pl.BlockSpec((1, bt, D), lambda b, j: (b, 0, 0)),
        pl.BlockSpec((1, bt, LRU_TAPS - 1, D), lambda b, j: (state_layer, b, 0, 0)),
        pl.BlockSpec((1, bt, CONF_TAPS - 1, D), lambda b, j: (state_layer, b, 0, 0)),
    ]
    in_specs += [_const_spec(c.shape) for c in consts]
    out_shape = (
        jax.ShapeDtypeStruct((nb, t_len, bt, D), F32),
        jax.ShapeDtypeStruct((nb * n_t, rows, D), BF16),
        jax.ShapeDtypeStruct((nb, n_t, 2 * N_EXPERTS, rows), F32),
        jax.ShapeDtypeStruct((nb, n_t, N_EXPERTS, 128), F32),
        jax.ShapeDtypeStruct((nb, bt, D), F32),
        jax.ShapeDtypeStruct((nb * bt, LRU_TAPS - 1, D), F32),
        jax.ShapeDtypeStruct((nb * bt, CONF_TAPS - 1, D), F32),
    )
    out_specs = (
        pl.BlockSpec((1, t_tile, bt, D), lambda b, j: (b, j, 0, 0)),
        pl.BlockSpec((1, rows, D), lambda b, j: (b * n_t + j, 0, 0)),
        pl.BlockSpec((1, 1, 2 * N_EXPERTS, rows), lambda b, j: (b, j, 0, 0)),
        pl.BlockSpec((1, 1, N_EXPERTS, 128), lambda b, j: (b, j, 0, 0)),
        pl.BlockSpec((1, bt, D), lambda b, j: (b, 0, 0)),
        pl.BlockSpec((bt, LRU_TAPS - 1, D), lambda b, j: (b, 0, 0)),
        pl.BlockSpec((bt, CONF_TAPS - 1, D), lambda b, j: (b, 0, 0)),
    )
    scratch = [
        pltpu.VMEM((t_tile + LRU_TAPS - 1, bt, D), F32),
        pltpu.VMEM((t_tile + CONF_TAPS - 1, bt, D), F32),
        pltpu.VMEM((t_tile, bt, D), F32),
        pltpu.VMEM((t_tile, bt, D), F32),
        pltpu.VMEM((bt, D), F32),
    ]
    return pl.pallas_call(
        kern,
        out_shape=out_shape,
        grid=(nb, n_t),
        in_specs=in_specs,
        out_specs=out_specs,
        scratch_shapes=scratch,
        compiler_params=pltpu.CompilerParams(
            dimension_semantics=("arbitrary", "arbitrary"), vmem_limit_bytes=VMEM_LIMIT),
        name="mix_reset" if reset else "mix_cont",
    )(x, mod, h0, lbuf, cbuf, *consts)


def _segment_issue(seg_len_ref, seg_start_ref, tile, make_copy):
    off = 0
    for e in range(N_EXPERTS):
        length = seg_len_ref[tile * N_EXPERTS + e]
        start = seg_start_ref[tile * N_EXPERTS + e]
        n_big = length // SEG_BLOCK
        done = n_big * SEG_BLOCK

        def issue_big(k, carry, off=off, start=start):
            local = pl.multiple_of(off + k * SEG_BLOCK, SEG_ALIGN)
            packed = pl.multiple_of(start + k * SEG_BLOCK, SEG_ALIGN)
            make_copy(local, packed, SEG_BLOCK).start()
            return carry

        def issue_small(k, carry, off=off + done, start=start + done):
            local = pl.multiple_of(off + k * SEG_ALIGN, SEG_ALIGN)
            packed = pl.multiple_of(start + k * SEG_ALIGN, SEG_ALIGN)
            make_copy(local, packed, SEG_ALIGN).start()
            return carry

        lax.fori_loop(0, n_big, issue_big, 0)
        lax.fori_loop(0, (length - done) // SEG_ALIGN, issue_small, 0)
        off = off + length


def _segment_wait(seg_len_ref, tile, make_copy):
    n_big, n_small = 0, 0
    for e in range(N_EXPERTS):
        length = seg_len_ref[tile * N_EXPERTS + e]
        n_big = n_big + length // SEG_BLOCK
        n_small = n_small + (length % SEG_BLOCK) // SEG_ALIGN

    def wait_big(k, carry):
        make_copy(0, 0, SEG_BLOCK).wait()
        return carry

    def wait_small(k, carry):
        make_copy(0, 0, SEG_ALIGN).wait()
        return carry

    lax.fori_loop(0, n_big, wait_big, 0)
    lax.fori_loop(0, n_small, wait_small, 0)


def _dispatch_kernel(seg_len_ref, seg_start_ref, tail_ref, *refs, chunk, group_tiles):
    hf_refs = refs[:len(group_tiles)]
    route_ref, xbuf_ref, meta_ref, xs_ref, zero_ref, sem = refs[len(group_tiles):]
    i = pl.program_id(0)
    last = pl.num_programs(0) - 1
    comb = route_ref[0, 0:N_EXPERTS, :]
    sel = route_ref[0, N_EXPERTS:2 * N_EXPERTS, :]
    on = sel > 0.0

    r_idx = lax.broadcasted_iota(I32, (MOE_TILE, MOE_TILE), 0)
    c_idx = lax.broadcasted_iota(I32, (MOE_TILE, MOE_TILE), 1)
    upper = jnp.where(r_idx <= c_idx, 1.0, 0.0).astype(BF16)
    csum = jnp.dot(sel.astype(BF16), upper, preferred_element_type=F32)
    rank = csum - sel
    count = csum[:, MOE_TILE - 1:MOE_TILE].astype(I32)
    padded = ((count + (SEG_ALIGN - 1)) // SEG_ALIGN) * SEG_ALIGN
    e_r = lax.broadcasted_iota(I32, (N_EXPERTS, N_EXPERTS), 0)
    e_c = lax.broadcasted_iota(I32, (N_EXPERTS, N_EXPERTS), 1)
    lower = jnp.where(e_c < e_r, 1.0, 0.0).astype(BF16)
    padded_b = jnp.broadcast_to(padded.astype(F32), (N_EXPERTS, 128)).astype(BF16)
    seg_off = jnp.dot(lower, padded_b, preferred_element_type=F32)[:, 0:1]

    pos = seg_off + rank
    p_lo = jnp.min(jnp.where(on, pos, 4.0 * SORTED_ROWS), axis=0, keepdims=True)
    p_hi = jnp.max(jnp.where(on, pos, -1.0), axis=0, keepdims=True)
    w_lo = jnp.sum(jnp.where(jnp.logical_and(on, pos == p_lo), comb, 0.0), axis=0, keepdims=True)
    w_hi = jnp.sum(jnp.where(jnp.logical_and(on, pos == p_hi), comb, 0.0), axis=0, keepdims=True)
    meta_ref[0] = jnp.concatenate(
        [p_lo, p_hi, w_lo, w_hi, jnp.zeros((4, MOE_TILE), F32)], axis=0)

    s_idx = lax.broadcasted_iota(I32, (SORTED_ROWS, MOE_TILE), 0)
    one_hot = jnp.where(s_idx == p_lo.astype(I32), 1.0,
                        jnp.where(s_idx == p_hi.astype(I32), 1.0, 0.0)).astype(BF16)
    hf = hf_refs[0][...]
    first_tile = group_tiles[0]
    for g in range(1, len(group_tiles)):
        hf = jnp.where(i >= first_tile, hf_refs[g][...], hf)
        first_tile += group_tiles[g]
    sorted_rows = jnp.dot(one_hot, hf, preferred_element_type=F32).astype(BF16)

    def seg_copy(local, packed, n_rows):
        return pltpu.make_async_copy(xs_ref.at[pl.ds(local, n_rows), :],
                                     xbuf_ref.at[pl.ds(packed, n_rows), :], sem.at[0])

    @pl.when(i > 0)
    def _():
        _segment_wait(seg_len_ref, i - 1, seg_copy)

    xs_ref[...] = sorted_rows
    _segment_issue(seg_len_ref, seg_start_ref, i, seg_copy)

    @pl.when(i == last)
    def _():
        _segment_wait(seg_len_ref, i, seg_copy)
        zero_ref[...] = jnp.zeros_like(zero_ref)

        def pad_copy(row):
            return pltpu.make_async_copy(zero_ref.at[pl.ds(0, SEG_ALIGN), :],
                                         xbuf_ref.at[pl.ds(row, SEG_ALIGN), :], sem.at[1])

        n_pad = 0
        for e in range(N_EXPERTS):
            def issue(k, carry, e=e):
                pad_copy(pl.multiple_of(tail_ref[e] + k * SEG_ALIGN, SEG_ALIGN)).start()
                return carry

            lax.fori_loop(0, tail_ref[N_EXPERTS + e], issue, 0)
            n_pad = n_pad + tail_ref[N_EXPERTS + e]

        def wait_pad(k, carry):
            pad_copy(0).wait()
            return carry

        lax.fori_loop(0, n_pad, wait_pad, 0)

        def idle_copy(c):
            return pltpu.make_async_copy(
                zero_ref, xbuf_ref.at[pl.ds(pl.multiple_of(c * chunk, chunk), chunk), :], sem.at[1])

        def issue_idle(c, carry):
            idle_copy(c).start()
            return carry

        def wait_idle(c, carry):
            idle_copy(c).wait()
            return carry

        n_chunks = xbuf_ref.shape[0] // chunk
        lax.fori_loop(tail_ref[2 * N_EXPERTS], n_chunks, issue_idle, 0)
        lax.fori_loop(tail_ref[2 * N_EXPERTS], n_chunks, wait_idle, 0)


def _dispatch_call(hf_groups, route, seg_len, seg_start, tail, *, chunk, n_chunks):
    n_tiles = route.shape[0]
    group_tiles = tuple(hf.shape[0] // MOE_TILE for hf in hf_groups)
    first_tiles = [sum(group_tiles[:g]) for g in range(len(group_tiles))]

    def hf_spec(first, count):
        return pl.BlockSpec((MOE_TILE, D), lambda i, *_: (jnp.clip(i - first, 0, count - 1), 0))

    grid_spec = pltpu.PrefetchScalarGridSpec(
        num_scalar_prefetch=3,
        grid=(n_tiles,),
        in_specs=[hf_spec(f, c) for f, c in zip(first_tiles, group_tiles)] + [
            pl.BlockSpec((1, 2 * N_EXPERTS, MOE_TILE), lambda i, *_: (i, 0, 0)),
        ],
        out_specs=(
            pl.BlockSpec(memory_space=pl.ANY),
            pl.BlockSpec((1, 8, MOE_TILE), lambda i, *_: (i, 0, 0)),
        ),
        scratch_shapes=[
            pltpu.VMEM((SORTED_ROWS, D), BF16),
            pltpu.VMEM((chunk, D), BF16),
            pltpu.SemaphoreType.DMA((2,)),
        ],
    )
    return pl.pallas_call(
        functools.partial(_dispatch_kernel, chunk=chunk, group_tiles=group_tiles),
        out_shape=(jax.ShapeDtypeStruct((n_chunks * chunk, D), BF16),
                   jax.ShapeDtypeStruct((n_tiles, 8, MOE_TILE), F32)),
        grid_spec=grid_spec,
        compiler_params=pltpu.CompilerParams(
            dimension_semantics=("arbitrary",), vmem_limit_bytes=VMEM_LIMIT),
        name="moe_dispatch",
    )(seg_len, seg_start, tail, *hf_groups, route)


def _expert_kernel(e_ref, x_idx_ref, valid_ref, x_ref, wg_ref, wu_ref, wd_ref, y_ref,
                   wg_bf, wu_bf, wd_bf):
    s = pl.program_id(0)

    @pl.when(jnp.logical_or(s == 0, e_ref[s] != e_ref[jnp.maximum(s - 1, 0)]))
    def _():
        wg_bf[...] = wg_ref[0, 0].astype(BF16)
        wu_bf[...] = wu_ref[0, 0].astype(BF16)
        wd_bf[...] = wd_ref[0, 0].astype(BF16)

    @pl.when(valid_ref[s] == 1)
    def _():
        x = x_ref[...]
        g = jnp.dot(x, wg_bf[...], preferred_element_type=F32)
        u = jnp.dot(x, wu_bf[...], preferred_element_type=F32)
        y = jnp.dot((_silu(g) * u).astype(BF16), wd_bf[...], preferred_element_type=F32)
        y_ref[...] = y.astype(BF16)

    @pl.when(valid_ref[s] == 0)
    def _():
        y_ref[...] = jnp.zeros_like(y_ref)


def _expert_call(xbuf, e_tab, x_idx_tab, valid_tab, lw, *, chunk):
    n_steps = e_tab.shape[0]
    layer = lw["layer"]
    grid_spec = pltpu.PrefetchScalarGridSpec(
        num_scalar_prefetch=3,
        grid=(n_steps,),
        in_specs=[
            pl.BlockSpec((chunk, D), lambda s, e, x, v: (x[s], 0)),
            pl.BlockSpec((1, 1, D, D_EXPERT), lambda s, e, x, v: (layer, e[s], 0, 0)),
            pl.BlockSpec((1, 1, D, D_EXPERT), lambda s, e, x, v: (layer, e[s], 0, 0)),
            pl.BlockSpec((1, 1, D_EXPERT, D), lambda s, e, x, v: (layer, e[s], 0, 0)),
        ],
        out_specs=pl.BlockSpec((chunk, D), lambda s, e, x, v: (s, 0)),
        scratch_shapes=[
            pltpu.VMEM((D, D_EXPERT), BF16),
            pltpu.VMEM((D, D_EXPERT), BF16),
            pltpu.VMEM((D_EXPERT, D), BF16),
        ],
    )
    return pl.pallas_call(
        _expert_kernel,
        out_shape=jax.ShapeDtypeStruct(xbuf.shape, BF16),
        grid_spec=grid_spec,
        compiler_params=pltpu.CompilerParams(
            dimension_semantics=("arbitrary",), vmem_limit_bytes=VMEM_LIMIT),
        name="moe_experts",
    )(e_tab, x_idx_tab, valid_tab, xbuf, lw["w_gate"], lw["w_up"], lw["w_down"])


def _combine_kernel(seg_len_ref, seg_start_ref, ybuf_ref, meta_ref, x1_ref, mod_ref, gfin_ref,
                    o_ref, ys_ref, sem, *, final, first_tile, batch_major_out):
    i = pl.program_id(0)
    n_tiles = pl.num_programs(0)
    slot = i % 2
    tile = first_tile + i

    def seg_copy(buf):
        def make(local, packed, n_rows):
            return pltpu.make_async_copy(ybuf_ref.at[pl.ds(packed, n_rows), :],
                                         ys_ref.at[buf, pl.ds(local, n_rows), :], sem.at[buf])
        return make

    @pl.when(i == 0)
    def _():
        ys_ref[...] = jnp.zeros_like(ys_ref)
        _segment_issue(seg_len_ref, seg_start_ref, tile, seg_copy(0))

    @pl.when(i + 1 < n_tiles)
    def _():
        _segment_issue(seg_len_ref, seg_start_ref, tile + 1, seg_copy(1 - slot))

    _segment_wait(seg_len_ref, tile, seg_copy(slot))

    meta = meta_ref[0]
    p_lo = meta[:, 0:1].astype(I32)
    p_hi = meta[:, 1:2].astype(I32)
    w_lo = meta[:, 2:3]
    w_hi = meta[:, 3:4]
    s_idx = lax.broadcasted_iota(I32, (MOE_TILE, SORTED_ROWS), 1)
    ys = ys_ref[slot]
    y_lo = jnp.dot(jnp.where(s_idx == p_lo, 1.0, 0.0).astype(BF16), ys, preferred_element_type=F32)
    y_hi = jnp.dot(jnp.where(s_idx == p_hi, 1.0, 0.0).astype(BF16), ys, preferred_element_type=F32)
    moe = w_lo * y_lo + w_hi * y_hi

    nbt, t_tile, bt, _ = x1_ref.shape
    gt2 = mod_ref[5].reshape(nbt, 1, bt, D)
    x2 = x1_ref[...] + gt2 * moe.reshape(nbt, t_tile, bt, D)
    if final:
        ms = jnp.mean(x2 * x2, axis=-1, keepdims=True)
        x2 = x2 * lax.rsqrt(ms + RMS_EPS) * gfin_ref[...]
    if batch_major_out:
        o_ref[...] = jnp.swapaxes(x2[0], 0, 1)
    else:
        o_ref[...] = x2


def _combine_call(ybuf, meta_t, x1, mod, g_final, seg_len, seg_start, *, final, first_tile,
                  batch_major_out):
    nb, t_len, bt, _ = x1.shape
    if nb * t_len * bt == MOE_TILE:
        nbt, t_tile = nb, t_len
    else:
        nbt, t_tile = 1, MOE_TILE // bt
    n_t = t_len // t_tile
    grid_spec = pltpu.PrefetchScalarGridSpec(
        num_scalar_prefetch=2,
        grid=((nb // nbt) * n_t,),
        in_specs=[
            pl.BlockSpec(memory_space=pl.ANY),
            pl.BlockSpec((1, MOE_TILE, 8), lambda i, *_: (i, 0, 0)),
            pl.BlockSpec((nbt, t_tile, bt, D), lambda i, *_: (i // n_t, i % n_t, 0, 0)),
            pl.BlockSpec((6, nbt * bt, D), lambda i, *_: (0, i // n_t, 0)),
            pl.BlockSpec((1, D), lambda i, *_: (0, 0)),
        ],
        out_specs=(pl.BlockSpec((bt, t_tile, D), lambda i, *_: (0, i, 0)) if batch_major_out else
                   pl.BlockSpec((nbt, t_tile, bt, D), lambda i, *_: (i // n_t, i % n_t, 0, 0))),
        scratch_shapes=[
            pltpu.VMEM((2, SORTED_ROWS, D), BF16),
            pltpu.SemaphoreType.DMA((2,)),
        ],
    )
    return pl.pallas_call(
        functools.partial(_combine_kernel, final=final, first_tile=first_tile,
                          batch_major_out=batch_major_out),
        out_shape=jax.ShapeDtypeStruct((bt, t_len, D) if batch_major_out else (nb, t_len, bt, D), F32),
        grid_spec=grid_spec,
        compiler_params=pltpu.CompilerParams(
            dimension_semantics=("arbitrary",), vmem_limit_bytes=VMEM_LIMIT),
        name="moe_combine_final" if final else "moe_combine",
    )(seg_len, seg_start, ybuf, meta_t, x1, mod, g_final)


def _moe_sparse(groups, lw, g_final, *, final):
    hf_groups = [g[0] for g in groups]
    group_tiles = [hf.shape[0] // MOE_TILE for hf in hf_groups]
    n_tok = sum(hf.shape[0] for hf in hf_groups)
    n_tiles = sum(group_tiles)
    def per_tile(r):
        if r.shape[-1] == MOE_TILE:
            return r.reshape(-1, 2 * N_EXPERTS, MOE_TILE)
        return r.transpose(0, 1, 3, 2).reshape(-1, MOE_TILE, 2 * N_EXPERTS).transpose(0, 2, 1)

    route = jnp.concatenate([per_tile(g[1]) for g in groups])

    count = jnp.concatenate([
        g[2][..., 0].astype(I32).reshape(t, -1, N_EXPERTS).sum(axis=1)
        for g, t in zip(groups, group_tiles)])
    chunk = EXPERT_CHUNK if n_tok >= N_EXPERTS * EXPERT_CHUNK else SMALL_EXPERT_CHUNK
    seg_len = (count + SEG_ALIGN - 1) // SEG_ALIGN * SEG_ALIGN
    total = jnp.sum(seg_len, axis=0)
    n_chunks = (total + chunk - 1) // chunk
    chunk_end = jnp.cumsum(n_chunks)
    used = chunk_end[-1]
    base = (chunk_end - n_chunks) * chunk
    seg_start = base[None, :] + jnp.cumsum(seg_len, axis=0) - seg_len
    max_chunks = -(-(TOP_K * n_tok + N_EXPERTS * SEG_ALIGN * n_tiles) // chunk) + N_EXPERTS
    step = jnp.arange(max_chunks, dtype=I32)
    valid = step < used
    x_idx = jnp.minimum(step, used - 1)
    e_tab = jnp.minimum(jnp.sum(x_idx[:, None] >= chunk_end[None, :], axis=1), N_EXPERTS - 1)
    tail = jnp.concatenate([base + total, (n_chunks * chunk - total) // SEG_ALIGN, used[None]])

    as_i32 = lambda v: v.reshape(-1).astype(I32)
    xbuf, meta = _dispatch_call(hf_groups, route, as_i32(seg_len), as_i32(seg_start), as_i32(tail),
                                chunk=chunk, n_chunks=max_chunks)
    ybuf = _expert_call(xbuf, as_i32(e_tab), as_i32(x_idx), as_i32(valid), lw, chunk=chunk)
    meta_t = meta.transpose(0, 2, 1)
    outs, first_tile = [], 0
    for (_, _, _, x1, mod), tiles in zip(groups, group_tiles):
        outs.append(_combine_call(ybuf, meta_t[first_tile:first_tile + tiles], x1, mod, g_final,
                                  as_i32(seg_len), as_i32(seg_start), final=final,
                                  first_tile=first_tile,
                                  batch_major_out=final and x1.shape[0] == 1))
        first_tile += tiles
    return outs


def _block_diag(w):
    heads_per_block = GATE_BLOCK // w.shape[-1]
    w = w.reshape(N_GATE_BLOCKS, heads_per_block, w.shape[-2], w.shape[-1])
    eye = jnp.eye(heads_per_block, dtype=w.dtype)
    out = jnp.einsum("ghij,hk->ghikj", w, eye)
    return out.reshape(N_GATE_BLOCKS, GATE_BLOCK, GATE_BLOCK).astype(BF16)


def _to_tiled(x, bt):
    b, t, d = x.shape
    return x.reshape(b // bt, bt, t, d).transpose(0, 2, 1, 3)


def _from_tiled(x):
    nb, t, bt, d = x.shape
    return x.transpose(0, 2, 1, 3).reshape(nb * bt, t, d)


def kernel(x_prompt, x_sample, state_lru_h, state_lru_conv, state_conf_conv, c_prompt, c_sample, w_ada, b_ada, g_mix, g_ffn, w_in, b_in, lru_conv_w, lru_conv_b, lru_wa, lru_ba, lru_wx, lru_bx, lru_lambda, w_lru_o, conf_conv_w, conf_conv_b, conf_ln_g, conf_ln_b, w_conf_o, w_out, w_router, b_router, w_gate, w_up, w_down, g_final):
    bp = x_prompt.shape[0]
    bs = x_sample.shape[0]
    btp = min(bp, MAX_BATCH_TILE)
    bts = min(bs, MAX_BATCH_TILE)
    row = lambda v: v.reshape(1, -1)
    sub8 = lambda w: jnp.broadcast_to(w[:, None, :], (w.shape[0], 8, w.shape[1]))

    c_all = jnp.concatenate([c_prompt, c_sample], axis=0)
    mod = _ada_call(c_all, w_ada, b_ada.reshape(DEPTH, 1, 6 * D))
    mod = mod.reshape(DEPTH, bp + bs, 6, D).transpose(0, 2, 1, 3)

    wr_t = w_router.T
    wr_hi = wr_t.astype(BF16)
    wr_lo = (wr_t - wr_hi.astype(F32)).astype(BF16)
    wr = jnp.concatenate([wr_hi, wr_lo], axis=0)
    g_fin = row(g_final)

    xp = x_prompt if bp == btp else _to_tiled(x_prompt, btp)
    xs = x_sample if bs == bts else _to_tiled(x_sample, bts)

    outs_p, outs_s = [], []
    for l in range(DEPTH):
        lw = {
            "g_mix": row(g_mix[l]), "g_ffn": row(g_ffn[l]),
            "w_in": w_in[l].astype(BF16), "b_in": row(b_in[l]),
            "lru_conv_w": sub8(lru_conv_w[l]), "lru_conv_b": row(lru_conv_b[l]),
            "wax": jnp.concatenate([_block_diag(lru_wa[l]), _block_diag(lru_wx[l])], axis=-1),
            "lru_ba": row(lru_ba[l]), "lru_bx": row(lru_bx[l]),
            "lru_lambda": row(lru_lambda[l]), "w_lru_o": w_lru_o[l].astype(BF16),
            "conf_conv_w": sub8(conf_conv_w[l]), "conf_conv_b": row(conf_conv_b[l]),
            "conf_ln_g": row(conf_ln_g[l]), "conf_ln_b": row(conf_ln_b[l]),
            "w_conf_o": w_conf_o[l].astype(BF16), "w_out": w_out[l].astype(BF16),
            "wr": wr, "b_router": b_router.reshape(N_EXPERTS, 1),
            "w_gate": w_gate, "w_up": w_up, "w_down": w_down, "layer": l,
        }
        final = l == DEPTH - 1
        groups = (
            (xp, mod[l, :, :bp], jnp.zeros((bp // btp, btp, D), F32),
             jnp.zeros((1, bp, LRU_TAPS - 1, D), F32),
             jnp.zeros((1, bp, CONF_TAPS - 1, D), F32), True, 0),
            (xs, mod[l, :, bp:], state_lru_h[l].reshape(bs // bts, bts, D),
             state_lru_conv, state_conf_conv, False, l),
        )
        moe_groups = []
        for (xg, modg, h0, lbuf, cbuf, reset, state_layer), outs in zip(groups, (outs_p, outs_s)):
            x1, hf, route, count, hlast, nlbuf, ncbuf = _mix_call(
                xg, modg, h0, lbuf, cbuf, lw, reset=reset, state_layer=state_layer)
            moe_groups.append((hf.reshape(-1, D), route, count, x1, modg))
            outs.append((hlast.reshape(-1, D), nlbuf, ncbuf))
        xp, xs = _moe_sparse(moe_groups, lw, g_fin, final=final)

    stack = lambda outs, k: jnp.stack([o[k] for o in outs])
    return (xp if xp.ndim == 3 else _from_tiled(xp), xs if xs.ndim == 3 else _from_tiled(xs),
            stack(outs_p, 0), stack(outs_p, 1), stack(outs_p, 2),
            stack(outs_s, 0), stack(outs_s, 1), stack(outs_s, 2))
```

```python
import functools

import jax
import jax.numpy as jnp
from jax import lax
from jax.experimental import pallas as pl
from jax.experimental.pallas import tpu as pltpu

F32 = jnp.float32
BF16 = jnp.bfloat16
I32 = jnp.int32

D = 1024
DEPTH = 2
N_EXPERTS = 16
EXPERTS_PER_GROUP = 4
N_GROUPS = N_EXPERTS // EXPERTS_PER_GROUP
TOP_K = 2
D_EXPERT = D // 2
LRU_TAPS = 4
CONF_TAPS = 31
LRU_C = 8.0
RMS_EPS = 1e-6
LN_EPS = 1e-5
GATE_BLOCK = 256
N_GATE_BLOCKS = D // GATE_BLOCK
MIX_ROWS = 512
MAX_BATCH_TILE = 32
CONV_CHUNK_ROWS = 128
CONV_CHUNK_LANES = 128
MOE_TILE = 512
SEG_ALIGN = 16
SEG_BLOCK = 64
SORTED_ROWS = TOP_K * MOE_TILE + N_EXPERTS * SEG_ALIGN
EXPERT_CHUNK = 512
SMALL_EXPERT_CHUNK = 128
VMEM_LIMIT = 56 * 1024 * 1024


def _sigmoid(x):
    return 1.0 / (1.0 + jnp.exp(-x))


def _silu(x):
    return x * _sigmoid(x)


def _gelu_tanh(x):
    c = 0.7978845608028654
    return 0.5 * x * (1.0 + jnp.tanh(c * (x + 0.044715 * (x * x * x))))


def _const_spec(shape):
    zeros = (0,) * len(shape)
    return pl.BlockSpec(shape, lambda *_: zeros, pipeline_mode=pl.Buffered(1))


def _ada_kernel(c_ref, w_ref, b_ref, o_ref):
    s = _silu(c_ref[...]).astype(BF16)
    o_ref[0] = jnp.dot(s, w_ref[0].astype(BF16), preferred_element_type=F32) + b_ref[0]


def _ada_call(c_all, w_ada, b_ada):
    nb = c_all.shape[0]
    tn = 1536
    return pl.pallas_call(
        _ada_kernel,
        out_shape=jax.ShapeDtypeStruct((DEPTH, nb, 6 * D), F32),
        grid=(DEPTH, 6 * D // tn),
        in_specs=[
            pl.BlockSpec((nb, D), lambda l, j: (0, 0)),
            pl.BlockSpec((1, D, tn), lambda l, j: (l, 0, j)),
            pl.BlockSpec((1, 1, tn), lambda l, j: (l, 0, j)),
        ],
        out_specs=pl.BlockSpec((1, nb, tn), lambda l, j: (l, 0, j)),
        name="adaln",
    )(c_all, w_ada, b_ada)


def _depthwise_conv(full_ref, w_ref, out_ref, *, taps, t_tile, batch):
    tc = max(1, CONV_CHUNK_ROWS // batch)
    n_chunks = t_tile // tc
    groups = tc * batch // 8
    for l0 in range(0, D, CONV_CHUNK_LANES):
        lanes = slice(l0, l0 + CONV_CHUNK_LANES)
        w = [w_ref[k, :, lanes] for k in range(taps)]

        def chunk(c, carry, lanes=lanes, w=w):
            t0 = c * tc
            acc = jnp.zeros((groups, 8, CONV_CHUNK_LANES), F32)
            for k in range(taps):
                rows = full_ref[pl.ds(t0 + k, tc), :, lanes].reshape(groups, 8, CONV_CHUNK_LANES)
                acc = acc + rows * w[k]
            out_ref[pl.ds(t0, tc), :, lanes] = acc.reshape(tc, batch, CONV_CHUNK_LANES)
            return carry

        lax.fori_loop(0, n_chunks, chunk, 0)


def _route(logits):
    m = logits[0]
    for e in range(1, N_EXPERTS):
        m = jnp.maximum(m, logits[e])
    ex = [jnp.exp(l - m) for l in logits]
    den = ex[0]
    for e in range(1, N_EXPERTS):
        den = den + ex[e]
    p = [v / den for v in ex]
    sums = []
    for g in range(N_GROUPS):
        q = p[g * EXPERTS_PER_GROUP:(g + 1) * EXPERTS_PER_GROUP]
        m1 = jnp.maximum(jnp.maximum(q[0], q[1]), jnp.maximum(q[2], q[3]))
        m2 = None
        for i in range(EXPERTS_PER_GROUP):
            for j in range(i + 1, EXPERTS_PER_GROUP):
                mn = jnp.minimum(q[i], q[j])
                m2 = mn if m2 is None else jnp.maximum(m2, mn)
        sums.append(m1 + m2)
    best = sums[0]
    grp = jnp.zeros(best.shape, I32)
    for g in range(1, N_GROUPS):
        take = sums[g] > best
        best = jnp.where(take, sums[g], best)
        grp = jnp.where(take, g, grp)
    combine, selected = [], []
    for g in range(N_GROUPS):
        q = p[g * EXPERTS_PER_GROUP:(g + 1) * EXPERTS_PER_GROUP]
        in_grp = grp == g
        for i in range(EXPERTS_PER_GROUP):
            rank = jnp.zeros(best.shape, I32)
            for j in range(EXPERTS_PER_GROUP):
                if j == i:
                    continue
                ahead = (q[j] > q[i]) if j > i else (q[j] >= q[i])
                rank = rank + ahead.astype(I32)
            sel = jnp.where(in_grp, rank, TOP_K) < TOP_K
            combine.append(jnp.where(sel, q[i] / best, 0.0))
            selected.append(jnp.where(sel, 1.0, 0.0))
    return combine, selected


def _mix_kernel(x_ref, mod_ref, h0_ref, lbuf_ref, cbuf_ref,
                gmix_ref, gffn_ref, win_ref, bin_ref,
                lcw_ref, lcb_ref, wax_ref, ba_ref, bx_ref, lam_ref, wlo_ref,
                ccw_ref, ccb_ref, lng_ref, lnb_ref, wco_ref, wout_ref,
                wr_ref, br_ref,
                x1_ref, hf_ref, route_ref, count_ref, hlast_ref, nlbuf_ref, ncbuf_ref,
                fu_ref, fg_ref, a_ref, b_ref, hcar_ref,
                *, t_tile, batch, reset, batch_major_in):
    j = pl.program_id(1)
    n_t = pl.num_programs(1)
    rows = t_tile * batch

    @pl.when(j == 0)
    def _():
        fu_ref[0:LRU_TAPS - 1] = jnp.swapaxes(lbuf_ref[0], 0, 1)
        fg_ref[0:CONF_TAPS - 1] = jnp.swapaxes(cbuf_ref[0], 0, 1)
        hcar_ref[...] = h0_ref[0]

    if batch_major_in:
        x = jnp.swapaxes(x_ref[...], 0, 1)
    else:
        x = x_ref[0]
    sh1, sc1, gt1 = mod_ref[0], mod_ref[1], mod_ref[2]
    sh2, sc2 = mod_ref[3], mod_ref[4]

    ms = jnp.mean(x * x, axis=-1, keepdims=True)
    hm = (x * lax.rsqrt(ms + RMS_EPS) * gmix_ref[...]) * (1.0 + sc1) + sh1
    hm2 = hm.reshape(rows, D).astype(BF16)

    def proj(k):
        cols = slice(k * D, (k + 1) * D)
        return jnp.dot(hm2, win_ref[:, cols], preferred_element_type=F32) + bin_ref[:, cols]

    fu_ref[LRU_TAPS - 1:] = proj(0).reshape(t_tile, batch, D)
    _depthwise_conv(fu_ref, lcw_ref, a_ref, taps=LRU_TAPS, t_tile=t_tile, batch=batch)
    xc = a_ref[...].reshape(rows, D) + lcb_ref[...]
    xcb = xc.astype(BF16)
    r_parts, i_parts = [], []
    for blk in range(N_GATE_BLOCKS):
        cols = slice(blk * GATE_BLOCK, (blk + 1) * GATE_BLOCK)
        both = jnp.dot(xcb[:, cols], wax_ref[blk], preferred_element_type=F32)
        r_parts.append(both[:, :GATE_BLOCK])
        i_parts.append(both[:, GATE_BLOCK:])
    r_gate = _sigmoid(jnp.concatenate(r_parts, axis=-1) + ba_ref[...])
    i_gate = _sigmoid(jnp.concatenate(i_parts, axis=-1) + bx_ref[...])
    neg_lam = -lam_ref[...]
    softplus = jnp.maximum(neg_lam, 0.0) + jnp.log1p(jnp.exp(-jnp.abs(neg_lam)))
    log_a = (-LRU_C) * r_gate * softplus
    a = jnp.exp(log_a)
    mult = jnp.sqrt(-jnp.tanh(log_a) * (a * a + 1.0))
    a = a.reshape(t_tile, batch, D)
    mult = mult.reshape(t_tile, batch, D)
    if reset:
        t_idx = lax.broadcasted_iota(I32, (t_tile, batch, D), 0)
        first = jnp.logical_and(t_idx == 0, j == 0)
        a = jnp.where(first, 0.0, a)
        mult = jnp.where(first, 1.0, mult)
    a_ref[...] = a
    b_ref[...] = mult * (i_gate * xc).reshape(t_tile, batch, D)

    def scan_step(t, carry):
        h = a_ref[t] * hcar_ref[...] + b_ref[t]
        hcar_ref[...] = h
        b_ref[t] = h
        return carry

    lax.fori_loop(0, t_tile, scan_step, 0)
    hseq = b_ref[...].reshape(rows, D)
    out_a = jnp.dot((hseq * _gelu_tanh(proj(1))).astype(BF16), wlo_ref[...],
                    preferred_element_type=F32)

    glu = proj(2) * _sigmoid(proj(3))
    fg_ref[CONF_TAPS - 1:] = glu.reshape(t_tile, batch, D)
    _depthwise_conv(fg_ref, ccw_ref, a_ref, taps=CONF_TAPS, t_tile=t_tile, batch=batch)
    dc = a_ref[...].reshape(rows, D) + ccb_ref[...]
    mu = jnp.mean(dc, axis=-1, keepdims=True)
    dcc = dc - mu
    var = jnp.mean(dcc * dcc, axis=-1, keepdims=True)
    ln = dcc * lax.rsqrt(var + LN_EPS) * lng_ref[...] + lnb_ref[...]
    out_b = jnp.dot(_silu(ln).astype(BF16), wco_ref[...], preferred_element_type=F32)

    merged = _sigmoid(proj(4)) * out_a + _sigmoid(proj(5)) * out_b
    mo = jnp.dot(merged.astype(BF16), wout_ref[...], preferred_element_type=F32)
    x1 = x + gt1 * mo.reshape(t_tile, batch, D)
    x1_ref[0] = x1

    ms2 = jnp.mean(x1 * x1, axis=-1, keepdims=True)
    hf = ((x1 * lax.rsqrt(ms2 + RMS_EPS) * gffn_ref[...]) * (1.0 + sc2) + sh2).reshape(rows, D)
    hf_hi = hf.astype(BF16)
    hf_ref[0] = hf_hi
    hf_lo = (hf - hf_hi.astype(F32)).astype(BF16)
    nt = (((1,), (1,)), ((), ()))
    by_hi = lax.dot_general(wr_ref[...], hf_hi, nt, preferred_element_type=F32)
    logits = (by_hi[:N_EXPERTS]
              + lax.dot_general(wr_ref[:N_EXPERTS], hf_lo, nt, preferred_element_type=F32)
              + by_hi[N_EXPERTS:]
              + br_ref[...])
    combine, selected = _route([logits[e:e + 1, :] for e in range(N_EXPERTS)])
    route_ref[0, 0] = jnp.concatenate(combine + selected, axis=0)
    n_sel = jnp.sum(jnp.concatenate(selected, axis=0), axis=-1, keepdims=True)
    count_ref[0, 0] = jnp.broadcast_to(n_sel, (N_EXPERTS, 128))

    fu_ref[0:LRU_TAPS - 1] = fu_ref[t_tile:t_tile + LRU_TAPS - 1]
    fg_ref[0:CONF_TAPS - 1] = fg_ref[t_tile:t_tile + CONF_TAPS - 1]

    @pl.when(j == n_t - 1)
    def _():
        hlast_ref[0] = hcar_ref[...]
        nlbuf_ref[...] = jnp.swapaxes(fu_ref[0:LRU_TAPS - 1], 0, 1)
        ncbuf_ref[...] = jnp.swapaxes(fg_ref[0:CONF_TAPS - 1], 0, 1)


def _mix_call(x, mod, h0, lbuf, cbuf, lw, *, reset, state_layer):
    batch_major_in = x.ndim == 3
    if batch_major_in:
        (bt, t_len, _), nb = x.shape, 1
    else:
        nb, t_len, bt, _ = x.shape
    t_tile = min(t_len, MIX_ROWS // bt)
    n_t = t_len // t_tile
    rows = t_tile * bt
    kern = functools.partial(_mix_kernel, t_tile=t_tile, batch=bt, reset=reset,
                             batch_major_in=batch_major_in)
    x_spec = (pl.BlockSpec((bt, t_tile, D), lambda b, j: (0, j, 0)) if batch_major_in else
              pl.BlockSpec((1, t_tile, bt, D), lambda b, j: (b, j, 0, 0)))
    consts = [lw["g_mix"], lw["g_ffn"], lw["w_in"], lw["b_in"],
              lw["lru_conv_w"], lw["lru_conv_b"], lw["wax"], lw["lru_ba"], lw["lru_bx"],
              lw["lru_lambda"], lw["w_lru_o"], lw["conf_conv_w"], lw["conf_conv_b"],
              lw["conf_ln_g"], lw["conf_ln_b"], lw["w_conf_o"], lw["w_out"],
              lw["wr"], lw["b_router"]]
    in_specs = [
        x_spec,
        pl.BlockSpec((6, bt, D), lambda b, j: (0, b, 0)),
        pl.BlockSpec((1, bt, D), lambda b, j: (b, 0, 0)),
        pl.BlockSpec((1, bt, LRU_TAPS - 1, D), lambda b, j: (state_layer, b, 0, 0)),
        pl.BlockSpec((1, bt, CONF_TAPS - 1, D), lambda b, j: (state_layer, b, 0, 0)),
    ]
    in_specs += [_const_spec(c.shape) for c in consts]
    out_shape = (
        jax.ShapeDtypeStruct((nb, t_len, bt, D), F32),
        jax.ShapeDtypeStruct((nb * n_t, rows, D), BF16),
        jax.ShapeDtypeStruct((nb, n_t, 2 * N_EXPERTS, rows), F32),
        jax.ShapeDtypeStruct((nb, n_t, N_EXPERTS, 128), F32),
        jax.ShapeDtypeStruct((nb, bt, D), F32),
        jax.ShapeDtypeStruct((nb * bt, LRU_TAPS - 1, D), F32),
        jax.ShapeDtypeStruct((nb * bt, CONF_TAPS - 1, D), F32),
    )
    out_specs = (
        pl.BlockSpec((1, t_tile, bt, D), lambda b, j: (b, j, 0, 0)),
        pl.BlockSpec((1, rows, D), lambda b, j: (b * n_t + j, 0, 0)),
        pl.BlockSpec((1, 1, 2 * N_EXPERTS, rows), lambda b, j: (b, j, 0, 0)),
        pl.BlockSpec((1, 1, N_EXPERTS, 128), lambda b, j: (b, j, 0, 0)),
        pl.BlockSpec((1, bt, D), lambda b, j: (b, 0, 0)),
        pl.BlockSpec((bt, LRU_TAPS - 1, D), lambda b, j: (b, 0, 0)),
        pl.BlockSpec((bt, CONF_TAPS - 1, D), lambda b, j: (b, 0, 0)),
    )
    scratch = [
        pltpu.VMEM((t_tile + LRU_TAPS - 1, bt, D), F32),
        pltpu.VMEM((t_tile + CONF_TAPS - 1, bt, D), F32),
        pltpu.VMEM((t_tile, bt, D), F32),
        pltpu.VMEM((t_tile, bt, D), F32),
        pltpu.VMEM((bt, D), F32),
    ]
    return pl.pallas_call(
        kern,
        out_shape=out_shape,
        grid=(nb, n_t),
        in_specs=in_specs,
        out_specs=out_specs,
        scratch_shapes=scratch,
        compiler_params=pltpu.CompilerParams(
            dimension_semantics=("arbitrary", "arbitrary"), vmem_limit_bytes=VMEM_LIMIT),
        name="mix_reset" if reset else "mix_cont",
    )(x, mod, h0, lbuf, cbuf, *consts)


def _segment_issue(seg_len_ref, seg_start_ref, tile, make_copy):
    off = 0
    for e in range(N_EXPERTS):
        length = seg_len_ref[tile * N_EXPERTS + e]
        start = seg_start_ref[tile * N_EXPERTS + e]
        n_big = length // SEG_BLOCK
        done = n_big * SEG_BLOCK

        def issue_big(k, carry, off=off, start=start):
            local = pl.multiple_of(off + k * SEG_BLOCK, SEG_ALIGN)
            packed = pl.multiple_of(start + k * SEG_BLOCK, SEG_ALIGN)
            make_copy(local, packed, SEG_BLOCK).start()
            return carry

        def issue_small(k, carry, off=off + done, start=start + done):
            local = pl.multiple_of(off + k * SEG_ALIGN, SEG_ALIGN)
            packed = pl.multiple_of(start + k * SEG_ALIGN, SEG_ALIGN)
            make_copy(local, packed, SEG_ALIGN).start()
            return carry

        lax.fori_loop(0, n_big, issue_big, 0)
        lax.fori_loop(0, (length - done) // SEG_ALIGN, issue_small, 0)
        off = off + length


def _segment_wait(seg_len_ref, tile, make_copy):
    n_big, n_small = 0, 0
    for e in range(N_EXPERTS):
        length = seg_len_ref[tile * N_EXPERTS + e]
        n_big = n_big + length // SEG_BLOCK
        n_small = n_small + (length % SEG_BLOCK) // SEG_ALIGN

    def wait_big(k, carry):
        make_copy(0, 0, SEG_BLOCK).wait()
        return carry

    def wait_small(k, carry):
        make_copy(0, 0, SEG_ALIGN).wait()
        return carry

    lax.fori_loop(0, n_big, wait_big, 0)
    lax.fori_loop(0, n_small, wait_small, 0)


def _dispatch_kernel(seg_len_ref, seg_start_ref, tail_ref, *refs, chunk, group_tiles):
    hf_refs = refs[:len(group_tiles)]
    route_ref, xbuf_ref, meta_ref, xs_ref, zero_ref, sem = refs[len(group_tiles):]
    i = pl.program_id(0)
    last = pl.num_programs(0) - 1
    comb = route_ref[0, 0:N_EXPERTS, :]
    sel = route_ref[0, N_EXPERTS:2 * N_EXPERTS, :]
    on = sel > 0.0

    r_idx = lax.broadcasted_iota(I32, (MOE_TILE, MOE_TILE), 0)
    c_idx = lax.broadcasted_iota(I32, (MOE_TILE, MOE_TILE), 1)
    upper = jnp.where(r_idx <= c_idx, 1.0, 0.0).astype(BF16)
    csum = jnp.dot(sel.astype(BF16), upper, preferred_element_type=F32)
    rank = csum - sel
    count = csum[:, MOE_TILE - 1:MOE_TILE].astype(I32)
    padded = ((count + (SEG_ALIGN - 1)) // SEG_ALIGN) * SEG_ALIGN
    e_r = lax.broadcasted_iota(I32, (N_EXPERTS, N_EXPERTS), 0)
    e_c = lax.broadcasted_iota(I32, (N_EXPERTS, N_EXPERTS), 1)
    lower = jnp.where(e_c < e_r, 1.0, 0.0).astype(BF16)
    padded_b = jnp.broadcast_to(padded.astype(F32), (N_EXPERTS, 128)).astype(BF16)
    seg_off = jnp.dot(lower, padded_b, preferred_element_type=F32)[:, 0:1]

    pos = seg_off + rank
    p_lo = jnp.min(jnp.where(on, pos, 4.0 * SORTED_ROWS), axis=0, keepdims=True)
    p_hi = jnp.max(jnp.where(on, pos, -1.0), axis=0, keepdims=True)
    w_lo = jnp.sum(jnp.where(jnp.logical_and(on, pos == p_lo), comb, 0.0), axis=0, keepdims=True)
    w_hi = jnp.sum(jnp.where(jnp.logical_and(on, pos == p_hi), comb, 0.0), axis=0, keepdims=True)
    meta_ref[0] = jnp.concatenate(
        [p_lo, p_hi, w_lo, w_hi, jnp.zeros((4, MOE_TILE), F32)], axis=0)

    s_idx = lax.broadcasted_iota(I32, (SORTED_ROWS, MOE_TILE), 0)
    one_hot = jnp.where(s_idx == p_lo.astype(I32), 1.0,
                        jnp.where(s_idx == p_hi.astype(I32), 1.0, 0.0)).astype(BF16)
    hf = hf_refs[0][...]
    first_tile = group_tiles[0]
    for g in range(1, len(group_tiles)):
        hf = jnp.where(i >= first_tile, hf_refs[g][...], hf)
        first_tile += group_tiles[g]
    sorted_rows = jnp.dot(one_hot, hf, preferred_element_type=F32).astype(BF16)

    def seg_copy(local, packed, n_rows):
        return pltpu.make_async_copy(xs_ref.at[pl.ds(local, n_rows), :],
                                     xbuf_ref.at[pl.ds(packed, n_rows), :], sem.at[0])

    @pl.when(i > 0)
    def _():
        _segment_wait(seg_len_ref, i - 1, seg_copy)

    xs_ref[...] = sorted_rows
    _segment_issue(seg_len_ref, seg_start_ref, i, seg_copy)

    @pl.when(i == last)
    def _():
        _segment_wait(seg_len_ref, i, seg_copy)
        zero_ref[...] = jnp.zeros_like(zero_ref)

        def pad_copy(row):
            return pltpu.make_async_copy(zero_ref.at[pl.ds(0, SEG_ALIGN), :],
                                         xbuf_ref.at[pl.ds(row, SEG_ALIGN), :], sem.at[1])

        n_pad = 0
        for e in range(N_EXPERTS):
            def issue(k, carry, e=e):
                pad_copy(pl.multiple_of(tail_ref[e] + k * SEG_ALIGN, SEG_ALIGN)).start()
                return carry

            lax.fori_loop(0, tail_ref[N_EXPERTS + e], issue, 0)
            n_pad = n_pad + tail_ref[N_EXPERTS + e]

        def wait_pad(k, carry):
            pad_copy(0).wait()
            return carry

        lax.fori_loop(0, n_pad, wait_pad, 0)

        def idle_copy(c):
            return pltpu.make_async_copy(
                zero_ref, xbuf_ref.at[pl.ds(pl.multiple_of(c * chunk, chunk), chunk), :], sem.at[1])

        def issue_idle(c, carry):
            idle_copy(c).start()
            return carry

        def wait_idle(c, carry):
            idle_copy(c).wait()
            return carry

        n_chunks = xbuf_ref.shape[0] // chunk
        lax.fori_loop(tail_ref[2 * N_EXPERTS], n_chunks, issue_idle, 0)
        lax.fori_loop(tail_ref[2 * N_EXPERTS], n_chunks, wait_idle, 0)


def _dispatch_call(hf_groups, route, seg_len, seg_start, tail, *, chunk, n_chunks):
    n_tiles = route.shape[0]
    group_tiles = tuple(hf.shape[0] // MOE_TILE for hf in hf_groups)
    first_tiles = [sum(group_tiles[:g]) for g in range(len(group_tiles))]

    def hf_spec(first, count):
        return pl.BlockSpec((MOE_TILE, D), lambda i, *_: (jnp.clip(i - first, 0, count - 1), 0))

    grid_spec = pltpu.PrefetchScalarGridSpec(
        num_scalar_prefetch=3,
        grid=(n_tiles,),
        in_specs=[hf_spec(f, c) for f, c in zip(first_tiles, group_tiles)] + [
            pl.BlockSpec((1, 2 * N_EXPERTS, MOE_TILE), lambda i, *_: (i, 0, 0)),
        ],
        out_specs=(
            pl.BlockSpec(memory_space=pl.ANY),
            pl.BlockSpec((1, 8, MOE_TILE), lambda i, *_: (i, 0, 0)),
        ),
        scratch_shapes=[
            pltpu.VMEM((SORTED_ROWS, D), BF16),
            pltpu.VMEM((chunk, D), BF16),
            pltpu.SemaphoreType.DMA((2,)),
        ],
    )
    return pl.pallas_call(
        functools.partial(_dispatch_kernel, chunk=chunk, group_tiles=group_tiles),
        out_shape=(jax.ShapeDtypeStruct((n_chunks * chunk, D), BF16),
                   jax.ShapeDtypeStruct((n_tiles, 8, MOE_TILE), F32)),
        grid_spec=grid_spec,
        compiler_params=pltpu.CompilerParams(
            dimension_semantics=("arbitrary",), vmem_limit_bytes=VMEM_LIMIT),
        name="moe_dispatch",
    )(seg_len, seg_start, tail, *hf_groups, route)


def _expert_kernel(e_ref, x_idx_ref, valid_ref, x_ref, wg_ref, wu_ref, wd_ref, y_ref,
                   wg_bf, wu_bf, wd_bf):
    s = pl.program_id(0)

    @pl.when(jnp.logical_or(s == 0, e_ref[s] != e_ref[jnp.maximum(s - 1, 0)]))
    def _():
        wg_bf[...] = wg_ref[0, 0].astype(BF16)
        wu_bf[...] = wu_ref[0, 0].astype(BF16)
        wd_bf[...] = wd_ref[0, 0].astype(BF16)

    @pl.when(valid_ref[s] == 1)
    def _():
        x = x_ref[...]
        g = jnp.dot(x, wg_bf[...], preferred_element_type=F32)
        u = jnp.dot(x, wu_bf[...], preferred_element_type=F32)
        y = jnp.dot((_silu(g) * u).astype(BF16), wd_bf[...], preferred_element_type=F32)
        y_ref[...] = y.astype(BF16)

    @pl.when(valid_ref[s] == 0)
    def _():
        y_ref[...] = jnp.zeros_like(y_ref)


def _expert_call(xbuf, e_tab, x_idx_tab, valid_tab, lw, *, chunk):
    n_steps = e_tab.shape[0]
    layer = lw["layer"]
    grid_spec = pltpu.PrefetchScalarGridSpec(
        num_scalar_prefetch=3,
        grid=(n_steps,),
        in_specs=[
            pl.BlockSpec((chunk, D), lambda s, e, x, v: (x[s], 0)),
            pl.BlockSpec((1, 1, D, D_EXPERT), lambda s, e, x, v: (layer, e[s], 0, 0)),
            pl.BlockSpec((1, 1, D, D_EXPERT), lambda s, e, x, v: (layer, e[s], 0, 0)),
            pl.BlockSpec((1, 1, D_EXPERT, D), lambda s, e, x, v: (layer, e[s], 0, 0)),
        ],
        out_specs=pl.BlockSpec((chunk, D), lambda s, e, x, v: (s, 0)),
        scratch_shapes=[
            pltpu.VMEM((D, D_EXPERT), BF16),
            pltpu.VMEM((D, D_EXPERT), BF16),
            pltpu.VMEM((D_EXPERT, D), BF16),
        ],
    )
    return pl.pallas_call(
        _expert_kernel,
        out_shape=jax.ShapeDtypeStruct(xbuf.shape, BF16),
        grid_spec=grid_spec,
        compiler_params=pltpu.CompilerParams(
            dimension_semantics=("arbitrary",), vmem_limit_bytes=VMEM_LIMIT),
        name="moe_experts",
    )(e_tab, x_idx_tab, valid_tab, xbuf, lw["w_gate"], lw["w_up"], lw["w_down"])


def _combine_kernel(seg_len_ref, seg_start_ref, ybuf_ref, meta_ref, x1_ref, mod_ref, gfin_ref,
                    o_ref, ys_ref, sem, *, final, first_tile, batch_major_out):
    i = pl.program_id(0)
    n_tiles = pl.num_programs(0)
    slot = i % 2
    tile = first_tile + i

    def seg_copy(buf):
        def make(local, packed, n_rows):
            return pltpu.make_async_copy(ybuf_ref.at[pl.ds(packed, n_rows), :],
                                         ys_ref.at[buf, pl.ds(local, n_rows), :], sem.at[buf])
        return make

    @pl.when(i == 0)
    def _():
        ys_ref[...] = jnp.zeros_like(ys_ref)
        _segment_issue(seg_len_ref, seg_start_ref, tile, seg_copy(0))

    @pl.when(i + 1 < n_tiles)
    def _():
        _segment_issue(seg_len_ref, seg_start_ref, tile + 1, seg_copy(1 - slot))

    _segment_wait(seg_len_ref, tile, seg_copy(slot))

    meta = meta_ref[0]
    p_lo = meta[:, 0:1].astype(I32)
    p_hi = meta[:, 1:2].astype(I32)
    w_lo = meta[:, 2:3]
    w_hi = meta[:, 3:4]
    s_idx = lax.broadcasted_iota(I32, (MOE_TILE, SORTED_ROWS), 1)
    ys = ys_ref[slot]
    y_lo = jnp.dot(jnp.where(s_idx == p_lo, 1.0, 0.0).astype(BF16), ys, preferred_element_type=F32)
    y_hi = jnp.dot(jnp.where(s_idx == p_hi, 1.0, 0.0).astype(BF16), ys, preferred_element_type=F32)
    moe = w_lo * y_lo + w_hi * y_hi

    nbt, t_tile, bt, _ = x1_ref.shape
    gt2 = mod_ref[5].reshape(nbt, 1, bt, D)
    x2 = x1_ref[...] + gt2 * moe.reshape(nbt, t_tile, bt, D)
    if final:
        ms = jnp.mean(x2 * x2, axis=-1, keepdims=True)
        x2 = x2 * lax.rsqrt(ms + RMS_EPS) * gfin_ref[...]
    if batch_major_out:
        o_ref[...] = jnp.swapaxes(x2[0], 0, 1)
    else:
        o_ref[...] = x2


def _combine_call(ybuf, meta_t, x1, mod, g_final, seg_len, seg_start, *, final, first_tile,
                  batch_major_out):
    nb, t_len, bt, _ = x1.shape
    if nb * t_len * bt == MOE_TILE:
        nbt, t_tile = nb, t_len
    else:
        nbt, t_tile = 1, MOE_TILE // bt
    n_t = t_len // t_tile
    grid_spec = pltpu.PrefetchScalarGridSpec(
        num_scalar_prefetch=2,
        grid=((nb // nbt) * n_t,),
        in_specs=[
            pl.BlockSpec(memory_space=pl.ANY),
            pl.BlockSpec((1, MOE_TILE, 8), lambda i, *_: (i, 0, 0)),
            pl.BlockSpec((nbt, t_tile, bt, D), lambda i, *_: (i // n_t, i % n_t, 0, 0)),
            pl.BlockSpec((6, nbt * bt, D), lambda i, *_: (0, i // n_t, 0)),
            pl.BlockSpec((1, D), lambda i, *_: (0, 0)),
        ],
        out_specs=(pl.BlockSpec((bt, t_tile, D), lambda i, *_: (0, i, 0)) if batch_major_out else
                   pl.BlockSpec((nbt, t_tile, bt, D), lambda i, *_: (i // n_t, i % n_t, 0, 0))),
        scratch_shapes=[
            pltpu.VMEM((2, SORTED_ROWS, D), BF16),
            pltpu.SemaphoreType.DMA((2,)),
        ],
    )
    return pl.pallas_call(
        functools.partial(_combine_kernel, final=final, first_tile=first_tile,
                          batch_major_out=batch_major_out),
        out_shape=jax.ShapeDtypeStruct((bt, t_len, D) if batch_major_out else (nb, t_len, bt, D), F32),
        grid_spec=grid_spec,
        compiler_params=pltpu.CompilerParams(
            dimension_semantics=("arbitrary",), vmem_limit_bytes=VMEM_LIMIT),
        name="moe_combine_final" if final else "moe_combine",
    )(seg_len, seg_start, ybuf, meta_t, x1, mod, g_final)


def _tables_kernel(count_ref, len_ref, start_ref, tail_ref, e_ref, x_idx_ref, valid_ref, *, chunk):
    n_tiles = count_ref.shape[0]
    max_chunks = e_ref.shape[0]

    def clear(s, carry):
        e_ref[s] = 0
        return carry

    lax.fori_loop(0, max_chunks, clear, 0)
    cursor = 0
    for e in range(N_EXPERTS):
        base = cursor * chunk

        def place(i, total, e=e, base=base):
            length = (count_ref[i, e] + (SEG_ALIGN - 1)) // SEG_ALIGN * SEG_ALIGN
            len_ref[i * N_EXPERTS + e] = length
            start_ref[i * N_EXPERTS + e] = base + total
            return total + length

        total = lax.fori_loop(0, n_tiles, place, 0)
        n_chunks = (total + (chunk - 1)) // chunk
        tail_ref[e] = base + total
        tail_ref[N_EXPERTS + e] = (n_chunks * chunk - total) // SEG_ALIGN

        def mark(s, carry, e=e):
            e_ref[s] = e
            return carry

        lax.fori_loop(cursor, cursor + n_chunks, mark, 0)
        cursor = cursor + n_chunks
    tail_ref[2 * N_EXPERTS] = cursor
    last_expert = e_ref[jnp.maximum(cursor - 1, 0)]

    def finish(s, carry):
        is_valid = s < cursor
        valid_ref[s] = is_valid.astype(I32)
        x_idx_ref[s] = jnp.minimum(s, cursor - 1)
        e_ref[s] = jnp.where(is_valid, e_ref[s], last_expert)
        return carry

    lax.fori_loop(0, max_chunks, finish, 0)


def _tables_call(count, *, chunk, max_chunks):
    n_tiles = count.shape[0]
    smem = pl.BlockSpec(memory_space=pltpu.SMEM)
    i32 = lambda n: jax.ShapeDtypeStruct((n,), I32)
    return pl.pallas_call(
        functools.partial(_tables_kernel, chunk=chunk),
        out_shape=(i32(n_tiles * N_EXPERTS), i32(n_tiles * N_EXPERTS), i32(2 * N_EXPERTS + 1),
                   i32(max_chunks), i32(max_chunks), i32(max_chunks)),
        in_specs=[smem],
        out_specs=(smem,) * 6,
        name="moe_tables",
    )(count)


def _moe_sparse(groups, lw, g_final, *, final):
    hf_groups = [g[0] for g in groups]
    group_tiles = [hf.shape[0] // MOE_TILE for hf in hf_groups]
    n_tok = sum(hf.shape[0] for hf in hf_groups)
    n_tiles = sum(group_tiles)
    def per_tile(r):
        if r.shape[-1] == MOE_TILE:
            return r.reshape(-1, 2 * N_EXPERTS, MOE_TILE)
        return r.transpose(0, 1, 3, 2).reshape(-1, MOE_TILE, 2 * N_EXPERTS).transpose(0, 2, 1)

    route = jnp.concatenate([per_tile(g[1]) for g in groups])

    count = jnp.concatenate([
        g[2][..., 0].astype(I32).reshape(t, -1, N_EXPERTS).sum(axis=1)
        for g, t in zip(groups, group_tiles)])
    chunk = EXPERT_CHUNK if n_tok >= N_EXPERTS * EXPERT_CHUNK else SMALL_EXPERT_CHUNK
    max_chunks = -(-(TOP_K * n_tok + N_EXPERTS * SEG_ALIGN * n_tiles) // chunk) + N_EXPERTS
    seg_len, seg_start, tail, e_tab, x_idx, valid = _tables_call(
        count, chunk=chunk, max_chunks=max_chunks)

    xbuf, meta = _dispatch_call(hf_groups, route, seg_len, seg_start, tail,
                                chunk=chunk, n_chunks=max_chunks)
    ybuf = _expert_call(xbuf, e_tab, x_idx, valid, lw, chunk=chunk)
    meta_t = meta.transpose(0, 2, 1)
    outs, first_tile = [], 0
    for (_, _, _, x1, mod), tiles in zip(groups, group_tiles):
        outs.append(_combine_call(ybuf, meta_t[first_tile:first_tile + tiles], x1, mod, g_final,
                                  seg_len, seg_start, final=final,
                                  first_tile=first_tile,
                                  batch_major_out=final and x1.shape[0] == 1))
        first_tile += tiles
    return outs


def _block_diag(w):
    head = w.shape[-1]
    heads_per_block = GATE_BLOCK // head
    rows = w.reshape(N_GATE_BLOCKS, GATE_BLOCK, head)
    band = jnp.tile(rows, (1, 1, heads_per_block))
    r = lax.broadcasted_iota(I32, (GATE_BLOCK, GATE_BLOCK), 0) // head
    c = lax.broadcasted_iota(I32, (GATE_BLOCK, GATE_BLOCK), 1) // head
    return jnp.where(r == c, band, 0.0).astype(BF16)


def _to_tiled(x, bt):
    b, t, d = x.shape
    return x.reshape(b // bt, bt, t, d).transpose(0, 2, 1, 3)


def _from_tiled(x):
    nb, t, bt, d = x.shape
    return x.transpose(0, 2, 1, 3).reshape(nb * bt, t, d)


def kernel(x_prompt, x_sample, state_lru_h, state_lru_conv, state_conf_conv, c_prompt, c_sample, w_ada, b_ada, g_mix, g_ffn, w_in, b_in, lru_conv_w, lru_conv_b, lru_wa, lru_ba, lru_wx, lru_bx, lru_lambda, w_lru_o, conf_conv_w, conf_conv_b, conf_ln_g, conf_ln_b, w_conf_o, w_out, w_router, b_router, w_gate, w_up, w_down, g_final):
    bp = x_prompt.shape[0]
    bs = x_sample.shape[0]
    btp = min(bp, MAX_BATCH_TILE)
    bts = min(bs, MAX_BATCH_TILE)
    row = lambda v: v.reshape(1, -1)
    sub8 = lambda w: jnp.broadcast_to(w[:, None, :], (w.shape[0], 8, w.shape[1]))

    c_all = jnp.concatenate([c_prompt, c_sample], axis=0)
    mod = _ada_call(c_all, w_ada, b_ada.reshape(DEPTH, 1, 6 * D))
    mod = mod.reshape(DEPTH, bp + bs, 6, D).transpose(0, 2, 1, 3)

    wr_t = w_router.T
    wr_hi = wr_t.astype(BF16)
    wr_lo = (wr_t - wr_hi.astype(F32)).astype(BF16)
    wr = jnp.concatenate([wr_hi, wr_lo], axis=0)
    g_fin = row(g_final)

    xp = x_prompt if bp == btp else _to_tiled(x_prompt, btp)
    xs = x_sample if bs == bts else _to_tiled(x_sample, bts)

    outs_p, outs_s = [], []
    for l in range(DEPTH):
        lw = {
            "g_mix": row(g_mix[l]), "g_ffn": row(g_ffn[l]),
            "w_in": w_in[l].astype(BF16), "b_in": row(b_in[l]),
            "lru_conv_w": sub8(lru_conv_w[l]), "lru_conv_b": row(lru_conv_b[l]),
            "wax": jnp.concatenate([_block_diag(lru_wa[l]), _block_diag(lru_wx[l])], axis=-1),
            "lru_ba": row(lru_ba[l]), "lru_bx": row(lru_bx[l]),
            "lru_lambda": row(lru_lambda[l]), "w_lru_o": w_lru_o[l].astype(BF16),
            "conf_conv_w": sub8(conf_conv_w[l]), "conf_conv_b": row(conf_conv_b[l]),
            "conf_ln_g": row(conf_ln_g[l]), "conf_ln_b": row(conf_ln_b[l]),
            "w_conf_o": w_conf_o[l].astype(BF16), "w_out": w_out[l].astype(BF16),
            "wr": wr, "b_router": b_router.reshape(N_EXPERTS, 1),
            "w_gate": w_gate, "w_up": w_up, "w_down": w_down, "layer": l,
        }
        final = l == DEPTH - 1
        groups = (
            (xp, mod[l, :, :bp], jnp.zeros((bp // btp, btp, D), F32),
             jnp.zeros((1, bp, LRU_TAPS - 1, D), F32),
             jnp.zeros((1, bp, CONF_TAPS - 1, D), F32), True, 0),
            (xs, mod[l, :, bp:], state_lru_h[l].reshape(bs // bts, bts, D),
             state_lru_conv, state_conf_conv, False, l),
        )
        moe_groups = []
        for (xg, modg, h0, lbuf, cbuf, reset, state_layer), outs in zip(groups, (outs_p, outs_s)):
            x1, hf, route, count, hlast, nlbuf, ncbuf = _mix_call(
                xg, modg, h0, lbuf, cbuf, lw, reset=reset, state_layer=state_layer)
            moe_groups.append((hf.reshape(-1, D), route, count, x1, modg))
            outs.append((hlast.reshape(-1, D), nlbuf, ncbuf))
        xp, xs = _moe_sparse(moe_groups, lw, g_fin, final=final)

    stack = lambda outs, k: jnp.stack([o[k] for o in outs])
    return (xp if xp.ndim == 3 else _from_tiled(xp), xs if xs.ndim == 3 else _from_tiled(xs),
            stack(outs_p, 0), stack(outs_p, 1), stack(outs_p, 2),
            stack(outs_s, 0), stack(outs_s, 1), stack(outs_s, 2))
```

```python
import functools

import jax
import jax.numpy as jnp
from jax import lax
from jax.experimental import pallas as pl
from jax.experimental.pallas import tpu as pltpu

F32 = jnp.float32
BF16 = jnp.bfloat16
I32 = jnp.int32

D = 1024
DEPTH = 2
N_EXPERTS = 16
EXPERTS_PER_GROUP = 4
N_GROUPS = N_EXPERTS // EXPERTS_PER_GROUP
TOP_K = 2
D_EXPERT = D // 2
LRU_TAPS = 4
CONF_TAPS = 31
LRU_C = 8.0
RMS_EPS = 1e-6
LN_EPS = 1e-5
GATE_BLOCK = 256
N_GATE_BLOCKS = D // GATE_BLOCK
MIX_ROWS = 512
MAX_BATCH_TILE = 32
CONV_CHUNK_ROWS = 128
CONV_CHUNK_LANES = 128
MOE_TILE = 512
SEG_ALIGN = 16
SEG_BLOCK = 64
SORTED_ROWS = TOP_K * MOE_TILE + N_EXPERTS * SEG_ALIGN
EXPERT_CHUNK = 512
SMALL_EXPERT_CHUNK = 128
VMEM_LIMIT = 56 * 1024 * 1024


def _sigmoid(x):
    return 1.0 / (1.0 + jnp.exp(-x))


def _silu(x):
    return x * _sigmoid(x)


def _gelu_tanh(x):
    c = 0.7978845608028654
    return 0.5 * x * (1.0 + jnp.tanh(c * (x + 0.044715 * (x * x * x))))


def _const_spec(shape):
    zeros = (0,) * len(shape)
    return pl.BlockSpec(shape, lambda *_: zeros, pipeline_mode=pl.Buffered(1))


def _layer_spec(shape, layer):
    index = (layer,) + (0,) * (len(shape) - 1)
    return pl.BlockSpec((1,) + tuple(shape[1:]), lambda *_: index, pipeline_mode=pl.Buffered(1))


def _ada_kernel(c_ref, w_ref, b_ref, o_ref):
    s = _silu(c_ref[...]).astype(BF16)
    o_ref[0] = jnp.dot(s, w_ref[0].astype(BF16), preferred_element_type=F32) + b_ref[0]


def _ada_call(c_all, w_ada, b_ada):
    nb = c_all.shape[0]
    tn = 1536
    return pl.pallas_call(
        _ada_kernel,
        out_shape=jax.ShapeDtypeStruct((DEPTH, nb, 6 * D), F32),
        grid=(DEPTH, 6 * D // tn),
        in_specs=[
            pl.BlockSpec((nb, D), lambda l, j: (0, 0)),
            pl.BlockSpec((1, D, tn), lambda l, j: (l, 0, j)),
            pl.BlockSpec((1, 1, tn), lambda l, j: (l, 0, j)),
        ],
        out_specs=pl.BlockSpec((1, nb, tn), lambda l, j: (l, 0, j)),
        name="adaln",
    )(c_all, w_ada, b_ada)


def _depthwise_conv(full_ref, w_ref, out_ref, *, taps, t_tile, batch):
    tc = max(1, CONV_CHUNK_ROWS // batch)
    n_chunks = t_tile // tc
    groups = tc * batch // 8
    for l0 in range(0, D, CONV_CHUNK_LANES):
        lanes = slice(l0, l0 + CONV_CHUNK_LANES)
        w = [w_ref[k, :, lanes] for k in range(taps)]

        def chunk(c, carry, lanes=lanes, w=w):
            t0 = c * tc
            acc = jnp.zeros((groups, 8, CONV_CHUNK_LANES), F32)
            for k in range(taps):
                rows = full_ref[pl.ds(t0 + k, tc), :, lanes].reshape(groups, 8, CONV_CHUNK_LANES)
                acc = acc + rows * w[k]
            out_ref[pl.ds(t0, tc), :, lanes] = acc.reshape(tc, batch, CONV_CHUNK_LANES)
            return carry

        lax.fori_loop(0, n_chunks, chunk, 0)


def _route(logits):
    m = logits[0]
    for e in range(1, N_EXPERTS):
        m = jnp.maximum(m, logits[e])
    ex = [jnp.exp(l - m) for l in logits]
    den = ex[0]
    for e in range(1, N_EXPERTS):
        den = den + ex[e]
    p = [v / den for v in ex]
    sums = []
    for g in range(N_GROUPS):
        q = p[g * EXPERTS_PER_GROUP:(g + 1) * EXPERTS_PER_GROUP]
        m1 = jnp.maximum(jnp.maximum(q[0], q[1]), jnp.maximum(q[2], q[3]))
        m2 = None
        for i in range(EXPERTS_PER_GROUP):
            for j in range(i + 1, EXPERTS_PER_GROUP):
                mn = jnp.minimum(q[i], q[j])
                m2 = mn if m2 is None else jnp.maximum(m2, mn)
        sums.append(m1 + m2)
    best = sums[0]
    grp = jnp.zeros(best.shape, I32)
    for g in range(1, N_GROUPS):
        take = sums[g] > best
        best = jnp.where(take, sums[g], best)
        grp = jnp.where(take, g, grp)
    combine, selected = [], []
    for g in range(N_GROUPS):
        q = p[g * EXPERTS_PER_GROUP:(g + 1) * EXPERTS_PER_GROUP]
        in_grp = grp == g
        for i in range(EXPERTS_PER_GROUP):
            rank = jnp.zeros(best.shape, I32)
            for j in range(EXPERTS_PER_GROUP):
                if j == i:
                    continue
                ahead = (q[j] > q[i]) if j > i else (q[j] >= q[i])
                rank = rank + ahead.astype(I32)
            sel = jnp.where(in_grp, rank, TOP_K) < TOP_K
            combine.append(jnp.where(sel, q[i] / best, 0.0))
            selected.append(jnp.where(sel, 1.0, 0.0))
    return combine, selected


def _mix_kernel(x_ref, mod_ref, h0_ref, lbuf_ref, cbuf_ref,
                gmix_ref, gffn_ref, win_ref, bin_ref,
                lcw_ref, lcb_ref, wax_ref, ba_ref, bx_ref, lam_ref, wlo_ref,
                ccw_ref, ccb_ref, lng_ref, lnb_ref, wco_ref, wout_ref,
                wr_ref, br_ref,
                x1_ref, hf_ref, route_ref, count_ref, hlast_ref, nlbuf_ref, ncbuf_ref,
                fu_ref, fg_ref, a_ref, b_ref, hcar_ref,
                *, t_tile, batch, reset, batch_major_in):
    j = pl.program_id(1)
    n_t = pl.num_programs(1)
    rows = t_tile * batch

    @pl.when(j == 0)
    def _():
        fu_ref[0:LRU_TAPS - 1] = jnp.swapaxes(lbuf_ref[0], 0, 1)
        fg_ref[0:CONF_TAPS - 1] = jnp.swapaxes(cbuf_ref[0], 0, 1)
        hcar_ref[...] = h0_ref[0]

    if batch_major_in:
        x = jnp.swapaxes(x_ref[...], 0, 1)
    else:
        x = x_ref[0]
    sh1, sc1, gt1 = mod_ref[0], mod_ref[1], mod_ref[2]
    sh2, sc2 = mod_ref[3], mod_ref[4]

    ms = jnp.mean(x * x, axis=-1, keepdims=True)
    hm = (x * lax.rsqrt(ms + RMS_EPS) * gmix_ref[...]) * (1.0 + sc1) + sh1
    hm2 = hm.reshape(rows, D).astype(BF16)

    def proj(k):
        cols = slice(k * D, (k + 1) * D)
        return jnp.dot(hm2, win_ref[0, :, cols], preferred_element_type=F32) + bin_ref[:, cols]

    fu_ref[LRU_TAPS - 1:] = proj(0).reshape(t_tile, batch, D)
    _depthwise_conv(fu_ref, lcw_ref, a_ref, taps=LRU_TAPS, t_tile=t_tile, batch=batch)
    xc = a_ref[...].reshape(rows, D) + lcb_ref[...]
    xcb = xc.astype(BF16)
    r_parts, i_parts = [], []
    for blk in range(N_GATE_BLOCKS):
        cols = slice(blk * GATE_BLOCK, (blk + 1) * GATE_BLOCK)
        both = jnp.dot(xcb[:, cols], wax_ref[blk], preferred_element_type=F32)
        r_parts.append(both[:, :GATE_BLOCK])
        i_parts.append(both[:, GATE_BLOCK:])
    r_gate = _sigmoid(jnp.concatenate(r_parts, axis=-1) + ba_ref[...])
    i_gate = _sigmoid(jnp.concatenate(i_parts, axis=-1) + bx_ref[...])
    neg_lam = -lam_ref[...]
    softplus = jnp.maximum(neg_lam, 0.0) + jnp.log1p(jnp.exp(-jnp.abs(neg_lam)))
    log_a = (-LRU_C) * r_gate * softplus
    a = jnp.exp(log_a)
    mult = jnp.sqrt(-jnp.tanh(log_a) * (a * a + 1.0))
    a = a.reshape(t_tile, batch, D)
    mult = mult.reshape(t_tile, batch, D)
    if reset:
        t_idx = lax.broadcasted_iota(I32, (t_tile, batch, D), 0)
        first = jnp.logical_and(t_idx == 0, j == 0)
        a = jnp.where(first, 0.0, a)
        mult = jnp.where(first, 1.0, mult)
    a_ref[...] = a
    b_ref[...] = mult * (i_gate * xc).reshape(t_tile, batch, D)

    def scan_step(t, carry):
        h = a_ref[t] * hcar_ref[...] + b_ref[t]
        hcar_ref[...] = h
        b_ref[t] = h
        return carry

    lax.fori_loop(0, t_tile, scan_step, 0)
    hseq = b_ref[...].reshape(rows, D)
    out_a = jnp.dot((hseq * _gelu_tanh(proj(1))).astype(BF16), wlo_ref[0],
                    preferred_element_type=F32)

    glu = proj(2) * _sigmoid(proj(3))
    fg_ref[CONF_TAPS - 1:] = glu.reshape(t_tile, batch, D)
    _depthwise_conv(fg_ref, ccw_ref, a_ref, taps=CONF_TAPS, t_tile=t_tile, batch=batch)
    dc = a_ref[...].reshape(rows, D) + ccb_ref[...]
    mu = jnp.mean(dc, axis=-1, keepdims=True)
    dcc = dc - mu
    var = jnp.mean(dcc * dcc, axis=-1, keepdims=True)
    ln = dcc * lax.rsqrt(var + LN_EPS) * lng_ref[...] + lnb_ref[...]
    out_b = jnp.dot(_silu(ln).astype(BF16), wco_ref[0], preferred_element_type=F32)

    merged = _sigmoid(proj(4)) * out_a + _sigmoid(proj(5)) * out_b
    mo = jnp.dot(merged.astype(BF16), wout_ref[0], preferred_element_type=F32)
    x1 = x + gt1 * mo.reshape(t_tile, batch, D)
    x1_ref[0] = x1

    ms2 = jnp.mean(x1 * x1, axis=-1, keepdims=True)
    hf = ((x1 * lax.rsqrt(ms2 + RMS_EPS) * gffn_ref[...]) * (1.0 + sc2) + sh2).reshape(rows, D)
    hf_hi = hf.astype(BF16)
    hf_ref[0] = hf_hi
    hf_lo = (hf - hf_hi.astype(F32)).astype(BF16)
    nt = (((1,), (1,)), ((), ()))
    by_hi = lax.dot_general(wr_ref[...], hf_hi, nt, preferred_element_type=F32)
    logits = (by_hi[:N_EXPERTS]
              + lax.dot_general(wr_ref[:N_EXPERTS], hf_lo, nt, preferred_element_type=F32)
              + by_hi[N_EXPERTS:]
              + br_ref[...])
    combine, selected = _route([logits[e:e + 1, :] for e in range(N_EXPERTS)])
    route_ref[0, 0] = jnp.concatenate(combine + selected, axis=0)
    n_sel = jnp.sum(jnp.concatenate(selected, axis=0), axis=-1, keepdims=True)
    count_ref[0, 0] = jnp.broadcast_to(n_sel, (N_EXPERTS, 128))

    fu_ref[0:LRU_TAPS - 1] = fu_ref[t_tile:t_tile + LRU_TAPS - 1]
    fg_ref[0:CONF_TAPS - 1] = fg_ref[t_tile:t_tile + CONF_TAPS - 1]

    @pl.when(j == n_t - 1)
    def _():
        hlast_ref[0] = hcar_ref[...]
        nlbuf_ref[...] = jnp.swapaxes(fu_ref[0:LRU_TAPS - 1], 0, 1)
        ncbuf_ref[...] = jnp.swapaxes(fg_ref[0:CONF_TAPS - 1], 0, 1)


def _mix_call(x, mod, h0, lbuf, cbuf, lw, *, reset, state_layer):
    batch_major_in = x.ndim == 3
    if batch_major_in:
        (bt, t_len, _), nb = x.shape, 1
    else:
        nb, t_len, bt, _ = x.shape
    t_tile = min(t_len, MIX_ROWS // bt)
    n_t = t_len // t_tile
    rows = t_tile * bt
    kern = functools.partial(_mix_kernel, t_tile=t_tile, batch=bt, reset=reset,
                             batch_major_in=batch_major_in)
    x_spec = (pl.BlockSpec((bt, t_tile, D), lambda b, j: (0, j, 0)) if batch_major_in else
              pl.BlockSpec((1, t_tile, bt, D), lambda b, j: (b, j, 0, 0)))
    const_names = ["g_mix", "g_ffn", "w_in", "b_in", "lru_conv_w", "lru_conv_b", "wax", "lru_ba",
                   "lru_bx", "lru_lambda", "w_lru_o", "conf_conv_w", "conf_conv_b", "conf_ln_g",
                   "conf_ln_b", "w_conf_o", "w_out", "wr", "b_router"]
    layer_stacks = ("w_in", "w_lru_o", "w_conf_o", "w_out")
    consts = [lw[k] for k in const_names]
    in_specs = [
        x_spec,
        pl.BlockSpec((6, bt, D), lambda b, j: (0, b, 0)),
        pl.BlockSpec((1, bt, D), lambda b, j: (b, 0, 0)),
        pl.BlockSpec((1, bt, LRU_TAPS - 1, D), lambda b, j: (state_layer, b, 0, 0)),
        pl.BlockSpec((1, bt, CONF_TAPS - 1, D), lambda b, j: (state_layer, b, 0, 0)),
    ]
    in_specs += [_layer_spec(lw[k].shape, lw["layer"]) if k in layer_stacks else _const_spec(lw[k].shape)
                 for k in const_names]
    out_shape = (
        jax.ShapeDtypeStruct((nb, t_len, bt, D), F32),
        jax.ShapeDtypeStruct((nb * n_t, rows, D), BF16),
        jax.ShapeDtypeStruct((nb, n_t, 2 * N_EXPERTS, rows), F32),
        jax.ShapeDtypeStruct((nb, n_t, N_EXPERTS, 128), F32),
        jax.ShapeDtypeStruct((nb, bt, D), F32),
        jax.ShapeDtypeStruct((nb * bt, LRU_TAPS - 1, D), F32),
        jax.ShapeDtypeStruct((nb * bt, CONF_TAPS - 1, D), F32),
    )
    out_specs = (
        pl.BlockSpec((1, t_tile, bt, D), lambda b, j: (b, j, 0, 0)),
        pl.BlockSpec((1, rows, D), lambda b, j: (b * n_t + j, 0, 0)),
        pl.BlockSpec((1, 1, 2 * N_EXPERTS, rows), lambda b, j: (b, j, 0, 0)),
        pl.BlockSpec((1, 1, N_EXPERTS, 128), lambda b, j: (b, j, 0, 0)),
        pl.BlockSpec((1, bt, D), lambda b, j: (b, 0, 0)),
        pl.BlockSpec((bt, LRU_TAPS - 1, D), lambda b, j: (b, 0, 0)),
        pl.BlockSpec((bt, CONF_TAPS - 1, D), lambda b, j: (b, 0, 0)),
    )
    scratch = [
        pltpu.VMEM((t_tile + LRU_TAPS - 1, bt, D), F32),
        pltpu.VMEM((t_tile + CONF_TAPS - 1, bt, D), F32),
        pltpu.VMEM((t_tile, bt, D), F32),
        pltpu.VMEM((t_tile, bt, D), F32),
        pltpu.VMEM((bt, D), F32),
    ]
    return pl.pallas_call(
        kern,
        out_shape=out_shape,
        grid=(nb, n_t),
        in_specs=in_specs,
        out_specs=out_specs,
        scratch_shapes=scratch,
        compiler_params=pltpu.CompilerParams(
            dimension_semantics=("arbitrary", "arbitrary"), vmem_limit_bytes=VMEM_LIMIT),
        name="mix_reset" if reset else "mix_cont",
    )(x, mod, h0, lbuf, cbuf, *consts)


def _segment_issue(seg_len_ref, seg_start_ref, tile, make_copy):
    off = 0
    for e in range(N_EXPERTS):
        length = seg_len_ref[tile * N_EXPERTS + e]
        start = seg_start_ref[tile * N_EXPERTS + e]
        n_big = length // SEG_BLOCK
        done = n_big * SEG_BLOCK

        def issue_big(k, carry, off=off, start=start):
            local = pl.multiple_of(off + k * SEG_BLOCK, SEG_ALIGN)
            packed = pl.multiple_of(start + k * SEG_BLOCK, SEG_ALIGN)
            make_copy(local, packed, SEG_BLOCK).start()
            return carry

        def issue_small(k, carry, off=off + done, start=start + done):
            local = pl.multiple_of(off + k * SEG_ALIGN, SEG_ALIGN)
            packed = pl.multiple_of(start + k * SEG_ALIGN, SEG_ALIGN)
            make_copy(local, packed, SEG_ALIGN).start()
            return carry

        lax.fori_loop(0, n_big, issue_big, 0)
        lax.fori_loop(0, (length - done) // SEG_ALIGN, issue_small, 0)
        off = off + length


def _segment_wait(seg_len_ref, tile, make_copy):
    n_big, n_small = 0, 0
    for e in range(N_EXPERTS):
        length = seg_len_ref[tile * N_EXPERTS + e]
        n_big = n_big + length // SEG_BLOCK
        n_small = n_small + (length % SEG_BLOCK) // SEG_ALIGN

    def wait_big(k, carry):
        make_copy(0, 0, SEG_BLOCK).wait()
        return carry

    def wait_small(k, carry):
        make_copy(0, 0, SEG_ALIGN).wait()
        return carry

    lax.fori_loop(0, n_big, wait_big, 0)
    lax.fori_loop(0, n_small, wait_small, 0)


def _dispatch_kernel(seg_len_ref, seg_start_ref, tail_ref, *refs, chunk, group_tiles):
    hf_refs = refs[:len(group_tiles)]
    route_ref, xbuf_ref, meta_ref, xs_ref, zero_ref, sem = refs[len(group_tiles):]
    i = pl.program_id(0)
    last = pl.num_programs(0) - 1
    comb = route_ref[0, 0:N_EXPERTS, :]
    sel = route_ref[0, N_EXPERTS:2 * N_EXPERTS, :]
    on = sel > 0.0

    r_idx = lax.broadcasted_iota(I32, (MOE_TILE, MOE_TILE), 0)
    c_idx = lax.broadcasted_iota(I32, (MOE_TILE, MOE_TILE), 1)
    upper = jnp.where(r_idx <= c_idx, 1.0, 0.0).astype(BF16)
    csum = jnp.dot(sel.astype(BF16), upper, preferred_element_type=F32)
    rank = csum - sel
    count = csum[:, MOE_TILE - 1:MOE_TILE].astype(I32)
    padded = ((count + (SEG_ALIGN - 1)) // SEG_ALIGN) * SEG_ALIGN
    e_r = lax.broadcasted_iota(I32, (N_EXPERTS, N_EXPERTS), 0)
    e_c = lax.broadcasted_iota(I32, (N_EXPERTS, N_EXPERTS), 1)
    lower = jnp.where(e_c < e_r, 1.0, 0.0).astype(BF16)
    padded_b = jnp.broadcast_to(padded.astype(F32), (N_EXPERTS, 128)).astype(BF16)
    seg_off = jnp.dot(lower, padded_b, preferred_element_type=F32)[:, 0:1]

    pos = seg_off + rank
    p_lo = jnp.min(jnp.where(on, pos, 4.0 * SORTED_ROWS), axis=0, keepdims=True)
    p_hi = jnp.max(jnp.where(on, pos, -1.0), axis=0, keepdims=True)
    w_lo = jnp.sum(jnp.where(jnp.logical_and(on, pos == p_lo), comb, 0.0), axis=0, keepdims=True)
    w_hi = jnp.sum(jnp.where(jnp.logical_and(on, pos == p_hi), comb, 0.0), axis=0, keepdims=True)
    meta_ref[0] = jnp.concatenate(
        [p_lo, p_hi, w_lo, w_hi, jnp.zeros((4, MOE_TILE), F32)], axis=0)

    s_idx = lax.broadcasted_iota(I32, (SORTED_ROWS, MOE_TILE), 0)
    one_hot = jnp.where(s_idx == p_lo.astype(I32), 1.0,
                        jnp.where(s_idx == p_hi.astype(I32), 1.0, 0.0)).astype(BF16)
    hf = hf_refs[0][...]
    first_tile = group_tiles[0]
    for g in range(1, len(group_tiles)):
        hf = jnp.where(i >= first_tile, hf_refs[g][...], hf)
        first_tile += group_tiles[g]
    sorted_rows = jnp.dot(one_hot, hf, preferred_element_type=F32).astype(BF16)

    def seg_copy(local, packed, n_rows):
        return pltpu.make_async_copy(xs_ref.at[pl.ds(local, n_rows), :],
                                     xbuf_ref.at[pl.ds(packed, n_rows), :], sem.at[0])

    @pl.when(i > 0)
    def _():
        _segment_wait(seg_len_ref, i - 1, seg_copy)

    xs_ref[...] = sorted_rows
    _segment_issue(seg_len_ref, seg_start_ref, i, seg_copy)

    @pl.when(i == last)
    def _():
        _segment_wait(seg_len_ref, i, seg_copy)
        zero_ref[...] = jnp.zeros_like(zero_ref)

        def pad_copy(row):
            return pltpu.make_async_copy(zero_ref.at[pl.ds(0, SEG_ALIGN), :],
                                         xbuf_ref.at[pl.ds(row, SEG_ALIGN), :], sem.at[1])

        n_pad = 0
        for e in range(N_EXPERTS):
            def issue(k, carry, e=e):
                pad_copy(pl.multiple_of(tail_ref[e] + k * SEG_ALIGN, SEG_ALIGN)).start()
                return carry

            lax.fori_loop(0, tail_ref[N_EXPERTS + e], issue, 0)
            n_pad = n_pad + tail_ref[N_EXPERTS + e]

        def wait_pad(k, carry):
            pad_copy(0).wait()
            return carry

        lax.fori_loop(0, n_pad, wait_pad, 0)

        def idle_copy(c):
            return pltpu.make_async_copy(
                zero_ref, xbuf_ref.at[pl.ds(pl.multiple_of(c * chunk, chunk), chunk), :], sem.at[1])

        def issue_idle(c, carry):
            idle_copy(c).start()
            return carry

        def wait_idle(c, carry):
            idle_copy(c).wait()
            return carry

        n_chunks = xbuf_ref.shape[0] // chunk
        lax.fori_loop(tail_ref[2 * N_EXPERTS], n_chunks, issue_idle, 0)
        lax.fori_loop(tail_ref[2 * N_EXPERTS], n_chunks, wait_idle, 0)


def _dispatch_call(hf_groups, route, seg_len, seg_start, tail, *, chunk, n_chunks):
    n_tiles = route.shape[0]
    group_tiles = tuple(hf.shape[0] // MOE_TILE for hf in hf_groups)
    first_tiles = [sum(group_tiles[:g]) for g in range(len(group_tiles))]

    def hf_spec(first, count):
        return pl.BlockSpec((MOE_TILE, D), lambda i, *_: (jnp.clip(i - first, 0, count - 1), 0))

    grid_spec = pltpu.PrefetchScalarGridSpec(
        num_scalar_prefetch=3,
        grid=(n_tiles,),
        in_specs=[hf_spec(f, c) for f, c in zip(first_tiles, group_tiles)] + [
            pl.BlockSpec((1, 2 * N_EXPERTS, MOE_TILE), lambda i, *_: (i, 0, 0)),
        ],
        out_specs=(
            pl.BlockSpec(memory_space=pl.ANY),
            pl.BlockSpec((1, 8, MOE_TILE), lambda i, *_: (i, 0, 0)),
        ),
        scratch_shapes=[
            pltpu.VMEM((SORTED_ROWS, D), BF16),
            pltpu.VMEM((chunk, D), BF16),
            pltpu.SemaphoreType.DMA((2,)),
        ],
    )
    return pl.pallas_call(
        functools.partial(_dispatch_kernel, chunk=chunk, group_tiles=group_tiles),
        out_shape=(jax.ShapeDtypeStruct((n_chunks * chunk, D), BF16),
                   jax.ShapeDtypeStruct((n_tiles, 8, MOE_TILE), F32)),
        grid_spec=grid_spec,
        compiler_params=pltpu.CompilerParams(
            dimension_semantics=("arbitrary",), vmem_limit_bytes=VMEM_LIMIT),
        name="moe_dispatch",
    )(seg_len, seg_start, tail, *hf_groups, route)


def _expert_kernel(e_ref, x_idx_ref, valid_ref, x_ref, wg_ref, wu_ref, wd_ref, y_ref,
                   wg_bf, wu_bf, wd_bf):
    s = pl.program_id(0)

    @pl.when(jnp.logical_or(s == 0, e_ref[s] != e_ref[jnp.maximum(s - 1, 0)]))
    def _():
        wg_bf[...] = wg_ref[0, 0].astype(BF16)
        wu_bf[...] = wu_ref[0, 0].astype(BF16)
        wd_bf[...] = wd_ref[0, 0].astype(BF16)

    @pl.when(valid_ref[s] == 1)
    def _():
        x = x_ref[...]
        g = jnp.dot(x, wg_bf[...], preferred_element_type=F32)
        u = jnp.dot(x, wu_bf[...], preferred_element_type=F32)
        y = jnp.dot((_silu(g) * u).astype(BF16), wd_bf[...], preferred_element_type=F32)
        y_ref[...] = y.astype(BF16)

    @pl.when(valid_ref[s] == 0)
    def _():
        y_ref[...] = jnp.zeros_like(y_ref)


def _expert_call(xbuf, e_tab, x_idx_tab, valid_tab, lw, *, chunk):
    n_steps = e_tab.shape[0]
    layer = lw["layer"]
    grid_spec = pltpu.PrefetchScalarGridSpec(
        num_scalar_prefetch=3,
        grid=(n_steps,),
        in_specs=[
            pl.BlockSpec((chunk, D), lambda s, e, x, v: (x[s], 0)),
            pl.BlockSpec((1, 1, D, D_EXPERT), lambda s, e, x, v: (layer, e[s], 0, 0)),
            pl.BlockSpec((1, 1, D, D_EXPERT), lambda s, e, x, v: (layer, e[s], 0, 0)),
            pl.BlockSpec((1, 1, D_EXPERT, D), lambda s, e, x, v: (layer, e[s], 0, 0)),
        ],
        out_specs=pl.BlockSpec((chunk, D), lambda s, e, x, v: (s, 0)),
        scratch_shapes=[
            pltpu.VMEM((D, D_EXPERT), BF16),
            pltpu.VMEM((D, D_EXPERT), BF16),
            pltpu.VMEM((D_EXPERT, D), BF16),
        ],
    )
    return pl.pallas_call(
        _expert_kernel,
        out_shape=jax.ShapeDtypeStruct(xbuf.shape, BF16),
        grid_spec=grid_spec,
        compiler_params=pltpu.CompilerParams(
            dimension_semantics=("arbitrary",), vmem_limit_bytes=VMEM_LIMIT),
        name="moe_experts",
    )(e_tab, x_idx_tab, valid_tab, xbuf, lw["w_gate"], lw["w_up"], lw["w_down"])


def _combine_kernel(seg_len_ref, seg_start_ref, ybuf_ref, meta_ref, x1_ref, mod_ref, gfin_ref,
                    o_ref, ys_ref, sem, *, final, first_tile, batch_major_out):
    i = pl.program_id(0)
    n_tiles = pl.num_programs(0)
    slot = i % 2
    tile = first_tile + i

    def seg_copy(buf):
        def make(local, packed, n_rows):
            return pltpu.make_async_copy(ybuf_ref.at[pl.ds(packed, n_rows), :],
                                         ys_ref.at[buf, pl.ds(local, n_rows), :], sem.at[buf])
        return make

    @pl.when(i == 0)
    def _():
        ys_ref[...] = jnp.zeros_like(ys_ref)
        _segment_issue(seg_len_ref, seg_start_ref, tile, seg_copy(0))

    @pl.when(i + 1 < n_tiles)
    def _():
        _segment_issue(seg_len_ref, seg_start_ref, tile + 1, seg_copy(1 - slot))

    _segment_wait(seg_len_ref, tile, seg_copy(slot))

    meta = meta_ref[0]
    p_lo = meta[:, 0:1].astype(I32)
    p_hi = meta[:, 1:2].astype(I32)
    w_lo = meta[:, 2:3]
    w_hi = meta[:, 3:4]
    s_idx = lax.broadcasted_iota(I32, (MOE_TILE, SORTED_ROWS), 1)
    ys = ys_ref[slot]
    y_lo = jnp.dot(jnp.where(s_idx == p_lo, 1.0, 0.0).astype(BF16), ys, preferred_element_type=F32)
    y_hi = jnp.dot(jnp.where(s_idx == p_hi, 1.0, 0.0).astype(BF16), ys, preferred_element_type=F32)
    moe = w_lo * y_lo + w_hi * y_hi

    nbt, t_tile, bt, _ = x1_ref.shape
    gt2 = mod_ref[5].reshape(nbt, 1, bt, D)
    x2 = x1_ref[...] + gt2 * moe.reshape(nbt, t_tile, bt, D)
    if final:
        ms = jnp.mean(x2 * x2, axis=-1, keepdims=True)
        x2 = x2 * lax.rsqrt(ms + RMS_EPS) * gfin_ref[...]
    if batch_major_out:
        o_ref[...] = jnp.swapaxes(x2[0], 0, 1)
    else:
        o_ref[...] = x2


def _combine_call(ybuf, meta_t, x1, mod, g_final, seg_len, seg_start, *, final, first_tile,
                  batch_major_out):
    nb, t_len, bt, _ = x1.shape
    if nb * t_len * bt == MOE_TILE:
        nbt, t_tile = nb, t_len
    else:
        nbt, t_tile = 1, MOE_TILE // bt
    n_t = t_len // t_tile
    grid_spec = pltpu.PrefetchScalarGridSpec(
        num_scalar_prefetch=2,
        grid=((nb // nbt) * n_t,),
        in_specs=[
            pl.BlockSpec(memory_space=pl.ANY),
            pl.BlockSpec((1, MOE_TILE, 8), lambda i, *_: (first_tile + i, 0, 0)),
            pl.BlockSpec((nbt, t_tile, bt, D), lambda i, *_: (i // n_t, i % n_t, 0, 0)),
            pl.BlockSpec((6, nbt * bt, D), lambda i, *_: (0, i // n_t, 0)),
            pl.BlockSpec((1, D), lambda i, *_: (0, 0)),
        ],
        out_specs=(pl.BlockSpec((bt, t_tile, D), lambda i, *_: (0, i, 0)) if batch_major_out else
                   pl.BlockSpec((nbt, t_tile, bt, D), lambda i, *_: (i // n_t, i % n_t, 0, 0))),
        scratch_shapes=[
            pltpu.VMEM((2, SORTED_ROWS, D), BF16),
            pltpu.SemaphoreType.DMA((2,)),
        ],
    )
    return pl.pallas_call(
        functools.partial(_combine_kernel, final=final, first_tile=first_tile,
                          batch_major_out=batch_major_out),
        out_shape=jax.ShapeDtypeStruct((bt, t_len, D) if batch_major_out else (nb, t_len, bt, D), F32),
        grid_spec=grid_spec,
        compiler_params=pltpu.CompilerParams(
            dimension_semantics=("arbitrary",), vmem_limit_bytes=VMEM_LIMIT),
        name="moe_combine_final" if final else "moe_combine",
    )(seg_len, seg_start, ybuf, meta_t, x1, mod, g_final)


def _tables_kernel(count_ref, len_ref, start_ref, tail_ref, e_ref, x_idx_ref, valid_ref, *, chunk):
    n_tiles = count_ref.shape[0]
    max_chunks = e_ref.shape[0]

    def clear(s, carry):
        e_ref[s] = 0
        return carry

    lax.fori_loop(0, max_chunks, clear, 0)
    cursor = 0
    for e in range(N_EXPERTS):
        base = cursor * chunk

        def place(i, total, e=e, base=base):
            length = (count_ref[i, e] + (SEG_ALIGN - 1)) // SEG_ALIGN * SEG_ALIGN
            len_ref[i * N_EXPERTS + e] = length
            start_ref[i * N_EXPERTS + e] = base + total
            return total + length

        total = lax.fori_loop(0, n_tiles, place, 0)
        n_chunks = (total + (chunk - 1)) // chunk
        tail_ref[e] = base + total
        tail_ref[N_EXPERTS + e] = (n_chunks * chunk - total) // SEG_ALIGN

        def mark(s, carry, e=e):
            e_ref[s] = e
            return carry

        lax.fori_loop(cursor, cursor + n_chunks, mark, 0)
        cursor = cursor + n_chunks
    tail_ref[2 * N_EXPERTS] = cursor
    last_expert = e_ref[jnp.maximum(cursor - 1, 0)]

    def finish(s, carry):
        is_valid = s < cursor
        valid_ref[s] = is_valid.astype(I32)
        x_idx_ref[s] = jnp.minimum(s, cursor - 1)
        e_ref[s] = jnp.where(is_valid, e_ref[s], last_expert)
        return carry

    lax.fori_loop(0, max_chunks, finish, 0)


def _tables_call(count, *, chunk, max_chunks):
    n_tiles = count.shape[0]
    smem = pl.BlockSpec(memory_space=pltpu.SMEM)
    i32 = lambda n: jax.ShapeDtypeStruct((n,), I32)
    return pl.pallas_call(
        functools.partial(_tables_kernel, chunk=chunk),
        out_shape=(i32(n_tiles * N_EXPERTS), i32(n_tiles * N_EXPERTS), i32(2 * N_EXPERTS + 1),
                   i32(max_chunks), i32(max_chunks), i32(max_chunks)),
        in_specs=[smem],
        out_specs=(smem,) * 6,
        name="moe_tables",
    )(count)


def _moe_sparse(groups, lw, g_final, *, final):
    hf_groups = [g[0] for g in groups]
    group_tiles = [hf.shape[0] // MOE_TILE for hf in hf_groups]
    n_tok = sum(hf.shape[0] for hf in hf_groups)
    n_tiles = sum(group_tiles)
    def per_tile(r):
        if r.shape[-1] == MOE_TILE:
            return r.reshape(-1, 2 * N_EXPERTS, MOE_TILE)
        return r.transpose(0, 1, 3, 2).reshape(-1, MOE_TILE, 2 * N_EXPERTS).transpose(0, 2, 1)

    route = jnp.concatenate([per_tile(g[1]) for g in groups])

    count = jnp.concatenate([
        g[2][..., 0].astype(I32).reshape(t, -1, N_EXPERTS).sum(axis=1)
        for g, t in zip(groups, group_tiles)])
    chunk = EXPERT_CHUNK if n_tok >= N_EXPERTS * EXPERT_CHUNK else SMALL_EXPERT_CHUNK
    max_chunks = -(-(TOP_K * n_tok + N_EXPERTS * SEG_ALIGN * n_tiles) // chunk) + N_EXPERTS
    seg_len, seg_start, tail, e_tab, x_idx, valid = _tables_call(
        count, chunk=chunk, max_chunks=max_chunks)

    xbuf, meta = _dispatch_call(hf_groups, route, seg_len, seg_start, tail,
                                chunk=chunk, n_chunks=max_chunks)
    ybuf = _expert_call(xbuf, e_tab, x_idx, valid, lw, chunk=chunk)
    meta_t = meta.transpose(0, 2, 1)
    outs, first_tile = [], 0
    for (_, _, _, x1, mod), tiles in zip(groups, group_tiles):
        outs.append(_combine_call(ybuf, meta_t, x1, mod, g_final,
                                  seg_len, seg_start, final=final,
                                  first_tile=first_tile,
                                  batch_major_out=final and x1.shape[0] == 1))
        first_tile += tiles
    return outs


def _block_diag(w):
    head = w.shape[-1]
    heads_per_block = GATE_BLOCK // head
    rows = w.reshape(N_GATE_BLOCKS, GATE_BLOCK, head)
    band = jnp.tile(rows, (1, 1, heads_per_block))
    r = lax.broadcasted_iota(I32, (GATE_BLOCK, GATE_BLOCK), 0) // head
    c = lax.broadcasted_iota(I32, (GATE_BLOCK, GATE_BLOCK), 1) // head
    return jnp.where(r == c, band, 0.0).astype(BF16)


def _to_tiled(x, bt):
    b, t, d = x.shape
    return x.reshape(b // bt, bt, t, d).transpose(0, 2, 1, 3)


def _from_tiled(x):
    nb, t, bt, d = x.shape
    return x.transpose(0, 2, 1, 3).reshape(nb * bt, t, d)


def kernel(x_prompt, x_sample, state_lru_h, state_lru_conv, state_conf_conv, c_prompt, c_sample, w_ada, b_ada, g_mix, g_ffn, w_in, b_in, lru_conv_w, lru_conv_b, lru_wa, lru_ba, lru_wx, lru_bx, lru_lambda, w_lru_o, conf_conv_w, conf_conv_b, conf_ln_g, conf_ln_b, w_conf_o, w_out, w_router, b_router, w_gate, w_up, w_down, g_final):
    bp = x_prompt.shape[0]
    bs = x_sample.shape[0]
    btp = min(bp, MAX_BATCH_TILE)
    bts = min(bs, MAX_BATCH_TILE)
    row = lambda v: v.reshape(1, -1)
    sub8 = lambda w: jnp.broadcast_to(w[:, None, :], (w.shape[0], 8, w.shape[1]))

    c_all = jnp.concatenate([c_prompt, c_sample], axis=0)
    mod = _ada_call(c_all, w_ada, b_ada.reshape(DEPTH, 1, 6 * D))
    mod = mod.reshape(DEPTH, bp + bs, 6, D).transpose(0, 2, 1, 3)

    wr_t = w_router.T
    wr_hi = wr_t.astype(BF16)
    wr_lo = (wr_t - wr_hi.astype(F32)).astype(BF16)
    wr = jnp.concatenate([wr_hi, wr_lo], axis=0)
    g_fin = row(g_final)

    xp = x_prompt if bp == btp else _to_tiled(x_prompt, btp)
    xs = x_sample if bs == bts else _to_tiled(x_sample, bts)

    w_in_bf, w_lru_o_bf = w_in.astype(BF16), w_lru_o.astype(BF16)
    w_conf_o_bf, w_out_bf = w_conf_o.astype(BF16), w_out.astype(BF16)

    outs_p, outs_s = [], []
    for l in range(DEPTH):
        lw = {
            "g_mix": row(g_mix[l]), "g_ffn": row(g_ffn[l]),
            "w_in": w_in_bf, "b_in": row(b_in[l]),
            "lru_conv_w": sub8(lru_conv_w[l]), "lru_conv_b": row(lru_conv_b[l]),
            "wax": jnp.concatenate([_block_diag(lru_wa[l]), _block_diag(lru_wx[l])], axis=-1),
            "lru_ba": row(lru_ba[l]), "lru_bx": row(lru_bx[l]),
            "lru_lambda": row(lru_lambda[l]), "w_lru_o": w_lru_o_bf,
            "conf_conv_w": sub8(conf_conv_w[l]), "conf_conv_b": row(conf_conv_b[l]),
            "conf_ln_g": row(conf_ln_g[l]), "conf_ln_b": row(conf_ln_b[l]),
            "w_conf_o": w_conf_o_bf, "w_out": w_out_bf,
            "wr": wr, "b_router": b_router.reshape(N_EXPERTS, 1),
            "w_gate": w_gate, "w_up": w_up, "w_down": w_down, "layer": l,
        }
        final = l == DEPTH - 1
        groups = (
            (xp, mod[l, :, :bp], jnp.zeros((bp // btp, btp, D), F32),
             jnp.zeros((1, bp, LRU_TAPS - 1, D), F32),
             jnp.zeros((1, bp, CONF_TAPS - 1, D), F32), True, 0),
            (xs, mod[l, :, bp:], state_lru_h[l].reshape(bs // bts, bts, D),
             state_lru_conv, state_conf_conv, False, l),
        )
        moe_groups = []
        for (xg, modg, h0, lbuf, cbuf, reset, state_layer), outs in zip(groups, (outs_p, outs_s)):
            x1, hf, route, count, hlast, nlbuf, ncbuf = _mix_call(
                xg, modg, h0, lbuf, cbuf, lw, reset=reset, state_layer=state_layer)
            moe_groups.append((hf.reshape(-1, D), route, count, x1, modg))
            outs.append((hlast.reshape(-1, D), nlbuf, ncbuf))
        xp, xs = _moe_sparse(moe_groups, lw, g_fin, final=final)

    stack = lambda outs, k: jnp.stack([o[k] for o in outs])
    return (xp if xp.ndim == 3 else _from_tiled(xp), xs if xs.ndim == 3 else _from_tiled(xs),
            stack(outs_p, 0), stack(outs_p, 1), stack(outs_p, 2),
            stack(outs_s, 0), stack(outs_s, 1), stack(outs_s, 2))
```
